```python
import math
import jax, jax.numpy as jnp
from jax import lax
import numpy as np

D_MODEL = 1024
BATCH = 2
SEQ = 8192
DEPTH = 4

ATT_HEADS = 8
KV_HEADS = 2
GQA_GROUP = ATT_HEADS // KV_HEADS
HEAD_DIM = 64
ATT_WIDTH = ATT_HEADS * HEAD_DIM
KV_WIDTH = KV_HEADS * HEAD_DIM
WINDOW = 128
BLOCK = 128
SSM_WIDTH = D_MODEL - ATT_WIDTH
SSM_GROUP = 16
SSM_GROUPS = SSM_WIDTH // SSM_GROUP
SSM_STATE = 64
DT_MIN = 1e-3
DT_MAX = 1e-1
MIX_WIDTH = ATT_WIDTH + SSM_WIDTH
IN_WIDTH = ATT_WIDTH + 2 * KV_WIDTH + SSM_WIDTH
D_FF = 4 * D_MODEL
EPS = 1e-6

kernel_name = "hymba_style_swa_s5_hybrid_encoder"


def rms_norm(x, gain):
    xf = x.astype(jnp.float32)
    y = xf * lax.rsqrt(jnp.mean(xf * xf, axis=-1, keepdims=True) + EPS)
    return (y * gain.astype(jnp.float32)).astype(x.dtype)


def alibi_slopes(n_heads):
    return jnp.exp2(-8.0 * jnp.arange(1, n_heads + 1, dtype=jnp.float32) / n_heads)


def windowed_gqa(q, k, v, q_gain, k_gain, sink):
    bsz, seq = q.shape[0], q.shape[1]
    nb = seq // BLOCK
    q = rms_norm(q, q_gain)
    k = rms_norm(k, k_gain)
    qb = q.reshape(bsz, nb, BLOCK, KV_HEADS, GQA_GROUP, HEAD_DIM)

    def band(t):
        tp = jnp.pad(t, ((0, 0), (BLOCK, BLOCK), (0, 0), (0, 0)))
        tb = tp.reshape(bsz, nb + 2, BLOCK, KV_HEADS, HEAD_DIM)
        return jnp.concatenate([tb[:, :-2], tb[:, 1:-1], tb[:, 2:]], axis=2)

    kb, vb = band(k), band(v)
    scores = jnp.einsum('bnqkgd,bnckd->bnkgqc', qb, kb,
                        preferred_element_type=jnp.float32) / math.sqrt(HEAD_DIM)
    q_idx = jnp.arange(BLOCK)[:, None]
    c_idx = jnp.arange(3 * BLOCK)[None, :]
    dist = jnp.abs(q_idx - c_idx + BLOCK)
    key_pos = (jnp.arange(nb)[:, None] - 1) * BLOCK + jnp.arange(3 * BLOCK)[None, :]
    valid = (dist <= WINDOW)[None] & ((key_pos >= 0) & (key_pos < seq))[:, None, :]
    slopes = alibi_slopes(ATT_HEADS).reshape(KV_HEADS, GQA_GROUP)
    bias = -slopes[:, :, None, None] * dist.astype(jnp.float32)
    neg = jnp.finfo(jnp.float32).min
    scores = jnp.where(valid[None, :, None, None], scores + bias, neg)
    sk = sink.astype(jnp.float32).reshape(1, 1, KV_HEADS, GQA_GROUP, 1, 1)
    m = jnp.maximum(jnp.max(scores, axis=-1, keepdims=True), sk)
    p = jnp.exp(scores - m)
    denom = jnp.sum(p, axis=-1, keepdims=True) + jnp.exp(sk - m)
    out = jnp.einsum('bnkgqc,bnckd->bnqkgd', (p / denom).astype(v.dtype), vb)
    return out.reshape(bsz, seq, ATT_WIDTH)


def complex_diag_scan(a_re, a_im, b_re, b_im, reverse):
    ar = jnp.broadcast_to(a_re, b_re.shape)
    ai = jnp.broadcast_to(a_im, b_re.shape)

    def combine(e1, e2):
        a1r, a1i, b1r, b1i = e1
        a2r, a2i, b2r, b2i = e2
        return (a1r * a2r - a1i * a2i,
                a1r * a2i + a1i * a2r,
                a2r * b1r - a2i * b1i + b2r,
                a2r * b1i + a2i * b1r + b2i)

    _, _, xr, xi = lax.associative_scan(combine, (ar, ai, b_re, b_im), reverse=reverse, axis=1)
    return xr, xi


def s5_mixer(u, lam_re, lam_im, log_dt, b_re, b_im, c_re, c_im, d_skip, w_glu):
    bsz, seq = u.shape[0], u.shape[1]
    uf = u.astype(jnp.float32).reshape(bsz, seq, SSM_GROUPS, SSM_GROUP)
    y = d_skip.astype(jnp.float32).reshape(SSM_GROUPS, SSM_GROUP) * uf
    br = b_re.astype(jnp.float32)
    bi = b_im.astype(jnp.float32)
    for direction, reverse in enumerate((False, True)):
        lr = lam_re[direction].astype(jnp.float32)
        li = lam_im[direction].astype(jnp.float32)
        dt = jnp.exp(log_dt[direction].astype(jnp.float32))[:, None]
        mag = jnp.exp(lr * dt)
        abr = mag * jnp.cos(li * dt)
        abi = mag * jnp.sin(li * dt)
        den = lr * lr + li * li
        zr = ((abr - 1.0) * lr + abi * li) / den
        zi = (abi * lr - (abr - 1.0) * li) / den
        bbr = zr[..., None] * br - zi[..., None] * bi
        bbi = zr[..., None] * bi + zi[..., None] * br
        bur = jnp.einsum('bsgh,gph->bsgp', uf, bbr)
        bui = jnp.einsum('bsgh,gph->bsgp', uf, bbi)
        xr, xi = complex_diag_scan(abr, abi, bur, bui, reverse)
        y = (y + jnp.einsum('bsgp,ghp->bsgh', xr, c_re[direction].astype(jnp.float32))
             - jnp.einsum('bsgp,ghp->bsgh', xi, c_im[direction].astype(jnp.float32)))
    y = jax.nn.gelu(y).reshape(bsz, seq, SSM_WIDTH).astype(u.dtype)
    g_val, g_gate = jnp.split(y @ w_glu, 2, axis=-1)
    return g_val * jax.nn.sigmoid(g_gate)


def setup_inputs(seed: int = 0) -> dict:
    key = jax.random.key(seed)
    ks = jax.random.split(key, 20)
    nrm = jax.random.normal
    f32 = jnp.float32
    x = nrm(ks[0], (BATCH, SEQ, D_MODEL), f32)
    norm1 = 1.0 + 0.05 * nrm(ks[1], (DEPTH, D_MODEL), f32)
    w_in = nrm(ks[2], (DEPTH, D_MODEL, IN_WIDTH), f32) * D_MODEL ** -0.5
    q_gain = 1.0 + 0.05 * nrm(ks[3], (DEPTH, HEAD_DIM), f32)
    k_gain = 1.0 + 0.05 * nrm(ks[4], (DEPTH, HEAD_DIM), f32)
    sink = 0.5 * nrm(ks[5], (DEPTH, ATT_HEADS), f32)
    lam_re = -0.5 + 0.01 * nrm(ks[6], (DEPTH, 2, SSM_GROUPS, SSM_STATE), f32)
    lam_im = (math.pi * jnp.arange(SSM_STATE, dtype=f32)
              + 0.01 * nrm(ks[7], (DEPTH, 2, SSM_GROUPS, SSM_STATE), f32))
    log_dt = jax.random.uniform(ks[8], (DEPTH, 2, SSM_GROUPS), f32,
                                minval=math.log(DT_MIN), maxval=math.log(DT_MAX))
    b_re = nrm(ks[9], (DEPTH, SSM_GROUPS, SSM_STATE, SSM_GROUP), f32) * (2 * SSM_GROUP) ** -0.5
    b_im = nrm(ks[10], (DEPTH, SSM_GROUPS, SSM_STATE, SSM_GROUP), f32) * (2 * SSM_GROUP) ** -0.5
    c_re = nrm(ks[11], (DEPTH, 2, SSM_GROUPS, SSM_GROUP, SSM_STATE), f32) * SSM_STATE ** -0.5
    c_im = nrm(ks[12], (DEPTH, 2, SSM_GROUPS, SSM_GROUP, SSM_STATE), f32) * SSM_STATE ** -0.5
    d_skip = nrm(ks[13], (DEPTH, SSM_WIDTH), f32)
    w_glu = nrm(ks[14], (DEPTH, SSM_WIDTH, 2 * SSM_WIDTH), f32) * SSM_WIDTH ** -0.5
    w_out = nrm(ks[15], (DEPTH, MIX_WIDTH, D_MODEL), f32) * (0.5 * MIX_WIDTH ** -0.5)
    norm2 = 1.0 + 0.05 * nrm(ks[16], (DEPTH, D_MODEL), f32)
    w_ff1 = nrm(ks[17], (DEPTH, D_MODEL, D_FF), f32) * D_MODEL ** -0.5
    w_ff2 = nrm(ks[18], (DEPTH, D_FF, D_MODEL), f32) * (0.5 * D_FF ** -0.5)
    return {"x": x, "norm1": norm1, "w_in": w_in, "q_gain": q_gain, "k_gain": k_gain,
            "sink": sink, "lam_re": lam_re, "lam_im": lam_im, "log_dt": log_dt,
            "b_re": b_re, "b_im": b_im, "c_re": c_re, "c_im": c_im, "d_skip": d_skip,
            "w_glu": w_glu, "w_out": w_out, "norm2": norm2, "w_ff1": w_ff1, "w_ff2": w_ff2}


def reference(x, norm1, w_in, q_gain, k_gain, sink, lam_re, lam_im, log_dt,
              b_re, b_im, c_re, c_im, d_skip, w_glu, w_out, norm2, w_ff1, w_ff2):
    bsz, seq = x.shape[0], x.shape[1]
    q_end = ATT_WIDTH
    k_end = q_end + KV_WIDTH
    v_end = k_end + KV_WIDTH
    for layer in range(DEPTH):
        h = rms_norm(x, norm1[layer])
        z = h @ w_in[layer]
        q = z[..., :q_end].reshape(bsz, seq, ATT_HEADS, HEAD_DIM)
        k = z[..., q_end:k_end].reshape(bsz, seq, KV_HEADS, HEAD_DIM)
        v = z[..., k_end:v_end].reshape(bsz, seq, KV_HEADS, HEAD_DIM)
        u = z[..., v_end:]
        att = windowed_gqa(q, k, v, q_gain[layer], k_gain[layer], sink[layer])
        ssm = s5_mixer(u, lam_re[layer], lam_im[layer], log_dt[layer], b_re[layer], b_im[layer],
                       c_re[layer], c_im[layer], d_skip[layer], w_glu[layer])
        x = x + jnp.concatenate([att, ssm], axis=-1) @ w_out[layer]
        h = rms_norm(x, norm2[layer])
        x = x + jnp.square(jax.nn.relu(h @ w_ff1[layer])) @ w_ff2[layer]
    return x
```

```python
import functools
import math

import jax
import jax.numpy as jnp
from jax import lax
from jax.experimental import pallas as pl
from jax.experimental.pallas import tpu as pltpu

D_MODEL = 1024
N_HEADS = 8
N_KV = 2
GQA = N_HEADS // N_KV
HEAD_DIM = 64
ATT_W = N_HEADS * HEAD_DIM
KV_W = N_KV * HEAD_DIM
QKV_W = ATT_W + 2 * KV_W
WINDOW = 128
SSM_W = 512
SSM_CH = 16
SSM_G = SSM_W // SSM_CH
SSM_P = 64
D_FF = 4 * D_MODEL
EPS = 1e-6

LANES = 128
SUB = 32
N_SUB = LANES // SUB
CW = SSM_CH * SUB
VMEM_LIMIT = 56 * 1024 * 1024

TM_IN = 512
TM_FFN = 512

_POW_N = (32, 64, 96, 128, 256, 512, 1024, 2048, 4096)
_POW_ROWS = 16
_N_SCAN_STEPS = 6

F32 = jnp.float32
BF16 = jnp.bfloat16


def _dot(a, b):
    return jnp.dot(a, b, preferred_element_type=F32)


def _dot_nt(a, b, precision=None):
    return lax.dot_general(a, b, (((1,), (1,)), ((), ())), precision=precision,
                           preferred_element_type=F32)


def _prep_kernel(lam_ref, bt_ref, c_ref, nrow_ref, t_ref, ms_ref, mo_ref, pw_ref,
                 g1r_ref, g1i_ref):
    s_col = lax.broadcasted_iota(jnp.int32, (SUB, SSM_P), 0).astype(F32)

    def powers(nmat, lrdt, lidt):
        mag = jnp.exp(nmat * lrdt)
        ang = nmat * lidt
        return mag * jnp.cos(ang), mag * jnp.sin(ang)

    kts = []
    for d in range(2):
        lr = lam_ref[0, 3 * d + 0:3 * d + 1, :]
        li = lam_ref[0, 3 * d + 1:3 * d + 2, :]
        dt = jnp.exp(lam_ref[0, 3 * d + 2:3 * d + 3, :])
        lrdt = lr * dt
        lidt = li * dt
        mag = jnp.exp(lrdt)
        abr = mag * jnp.cos(lidt)
        abi = mag * jnp.sin(lidt)
        den = lr * lr + li * li
        zr = ((abr - 1.0) * lr + abi * li) / den
        zi = (abi * lr - (abr - 1.0) * li) / den
        btr = bt_ref[0, 0]
        bti = bt_ref[0, 1]
        bbr = zr * btr - zi * bti
        bbi = zr * bti + zi * btr
        cr = c_ref[0, 2 * d + 0]
        ci = c_ref[0, 2 * d + 1]

        if d == 0:
            n_ms, n_t, n_mo = 31.0 - s_col, s_col, s_col + 1.0
        else:
            n_ms, n_t, n_mo = s_col, 31.0 - s_col, 32.0 - s_col
        msr, msi = powers(n_ms, lrdt, lidt)
        tr, ti = powers(n_t, lrdt, lidt)
        mor, moi = powers(n_mo, lrdt, lidt)

        for h in range(SSM_CH):
            rows = slice(h * SUB, (h + 1) * SUB)
            b_r = bbr[h:h + 1, :]
            b_i = bbi[h:h + 1, :]
            ms_ref[0, 2 * d + 0, rows, :] = msr * b_r - msi * b_i
            ms_ref[0, 2 * d + 1, rows, :] = msr * b_i + msi * b_r
            c_r = cr[h:h + 1, :]
            c_i = ci[h:h + 1, :]
            g1r_ref[rows, :] = tr * c_r - ti * c_i
            g1i_ref[rows, :] = tr * c_i + ti * c_r
            mo_ref[0, 2 * d + 0, rows, :] = mor * c_r - moi * c_i
            mo_ref[0, 2 * d + 1, rows, :] = -(mor * c_i + moi * c_r)

        kt = (_dot_nt(bbr, g1r_ref[...], precision=lax.Precision.HIGHEST)
              - _dot_nt(bbi, g1i_ref[...], precision=lax.Precision.HIGHEST))
        kts.append(kt)

        nmat = nrow_ref[...]
        pr, pi = powers(nmat, lrdt, lidt)
        pw_ref[0, 3 * d + 0] = pr
        pw_ref[0, 3 * d + 1] = -pi
        pw_ref[0, 3 * d + 2] = pi

    s_lane = lax.broadcasted_iota(jnp.int32, (SUB, CW), 1) & (SUB - 1)
    s_row = lax.broadcasted_iota(jnp.int32, (SUB, CW), 0)
    fwd_mask = s_lane >= s_row
    bwd_mask = s_lane <= s_row
    for h in range(SSM_CH):
        kf = jnp.broadcast_to(kts[0][h:h + 1, :], (SUB, CW))
        kb = jnp.broadcast_to(kts[1][h:h + 1, :], (SUB, CW))
        rf = pltpu.roll(kf, 0, 1, stride=1, stride_axis=0)
        rb = pltpu.roll(kb, CW - (SUB - 1), 1, stride=1, stride_axis=0)
        blk = jnp.where(fwd_mask, rf, 0.0) + jnp.where(bwd_mask, rb, 0.0)
        t_ref[0, h * SUB:(h + 1) * SUB, :] = blk.astype(BF16)


def _ssm_prep(lam, bt, c, nrow):
    lg = lam.shape[0]
    return pl.pallas_call(
        _prep_kernel,
        name="ssm_prep",
        grid=(lg,),
        in_specs=[
            pl.BlockSpec((1, 6, SSM_P), lambda i: (i, 0, 0)),
            pl.BlockSpec((1, 2, SSM_CH, SSM_P), lambda i: (i, 0, 0, 0)),
            pl.BlockSpec((1, 4, SSM_CH, SSM_P), lambda i: (i, 0, 0, 0)),
            pl.BlockSpec((_POW_ROWS, SSM_P), lambda i: (0, 0)),
        ],
        out_specs=[
            pl.BlockSpec((1, CW, CW), lambda i: (i, 0, 0)),
            pl.BlockSpec((1, 4, CW, SSM_P), lambda i: (i, 0, 0, 0)),
            pl.BlockSpec((1, 4, CW, SSM_P), lambda i: (i, 0, 0, 0)),
            pl.BlockSpec((1, 6, _POW_ROWS, SSM_P), lambda i: (i, 0, 0, 0)),
        ],
        out_shape=[
            jax.ShapeDtypeStruct((lg, CW, CW), BF16),
            jax.ShapeDtypeStruct((lg, 4, CW, SSM_P), F32),
            jax.ShapeDtypeStruct((lg, 4, CW, SSM_P), F32),
            jax.ShapeDtypeStruct((lg, 6, _POW_ROWS, SSM_P), F32),
        ],
        scratch_shapes=[pltpu.VMEM((CW, SSM_P), F32), pltpu.VMEM((CW, SSM_P), F32)],
        compiler_params=pltpu.CompilerParams(dimension_semantics=("arbitrary",)),
    )(lam, bt, c, nrow)


def _inproj_kernel(x_ref, g1_ref, wqkv_ref, wut_ref, qg_ref, kg_ref, bdq_ref, bdk_ref,
                   q_ref, k_ref, v_ref, u_ref):
    x = x_ref[...]
    ms = jnp.mean(x * x, axis=-1, keepdims=True)
    hn = (x * lax.rsqrt(ms + EPS) * g1_ref[...]).astype(BF16)
    z = _dot(hn, wqkv_ref[...])

    def head_norm(t, bd_ref, gain):
        t2 = t * t
        hi = t2.astype(BF16)
        lo = (t2 - hi.astype(F32)).astype(BF16)
        m = _dot(hi, bd_ref[...]) + _dot(lo, bd_ref[...])
        return t * lax.rsqrt(m + EPS) * gain

    q_ref[...] = head_norm(z[:, :ATT_W], bdq_ref, qg_ref[...]).astype(BF16)
    k_ref[...] = head_norm(z[:, ATT_W:ATT_W + KV_W], bdk_ref, kg_ref[...]).astype(BF16)
    v_ref[...] = z[:, ATT_W + KV_W:].astype(BF16)
    ut = _dot_nt(wut_ref[...], hn)
    for j in range(TM_IN // LANES):
        u_ref[j] = ut[:, j * LANES:(j + 1) * LANES]


def _inproj(x, g1, wqkv, wut, qg, kg, bdq, bdk):
    t = x.shape[0]
    const = lambda i: (0, 0)
    return pl.pallas_call(
        _inproj_kernel,
        name="inproj",
        grid=(t // TM_IN,),
        in_specs=[
            pl.BlockSpec((TM_IN, D_MODEL), lambda i: (i, 0)),
            pl.BlockSpec((1, D_MODEL), const),
            pl.BlockSpec((D_MODEL, QKV_W), const),
            pl.BlockSpec((SSM_W, D_MODEL), const),
            pl.BlockSpec((1, ATT_W), const),
            pl.BlockSpec((1, KV_W), const),
            pl.BlockSpec((ATT_W, ATT_W), const),
            pl.BlockSpec((KV_W, KV_W), const),
        ],
        out_specs=[
            pl.BlockSpec((TM_IN, ATT_W), lambda i: (i, 0)),
            pl.BlockSpec((TM_IN, KV_W), lambda i: (i, 0)),
            pl.BlockSpec((TM_IN, KV_W), lambda i: (i, 0)),
            pl.BlockSpec((TM_IN // LANES, SSM_W, LANES), lambda i: (i, 0, 0)),
        ],
        out_shape=[
            jax.ShapeDtypeStruct((t, ATT_W), BF16),
            jax.ShapeDtypeStruct((t, KV_W), BF16),
            jax.ShapeDtypeStruct((t, KV_W), BF16),
            jax.ShapeDtypeStruct((t // LANES, SSM_W, LANES), F32),
        ],
        compiler_params=pltpu.CompilerParams(dimension_semantics=("arbitrary",),
                                             vmem_limit_bytes=VMEM_LIMIT),
    )(x, g1, wqkv, wut, qg, kg, bdq, bdk)


def _attn_kernel(sink_ref, q_ref, k_ref, v_ref, bias_ref, o_ref, *, n_blocks):
    n = pl.program_id(1)
    prev = pl.multiple_of(jnp.maximum(n - 1, 0) * WINDOW, WINDOW)
    cur = pl.multiple_of(n * WINDOW, WINDOW)
    nxt = pl.multiple_of(jnp.minimum(n + 1, n_blocks - 1) * WINDOW, WINDOW)

    def window(ref):
        return jnp.concatenate([ref[0, pl.ds(prev, WINDOW), :], ref[0, pl.ds(cur, WINDOW), :],
                                ref[0, pl.ds(nxt, WINDOW), :]], axis=0)

    kw = window(k_ref)
    vw = window(v_ref)
    q = q_ref[...]
    outs = []
    for h in range(N_HEADS):
        j = h // GQA
        kj = kw[:, j * HEAD_DIM:(j + 1) * HEAD_DIM]
        vj = vw[:, j * HEAD_DIM:(j + 1) * HEAD_DIM]
        qh = q[:, h * HEAD_DIM:(h + 1) * HEAD_DIM]
        s = _dot_nt(qh, kj) + bias_ref[0, h]
        sk = sink_ref[h]
        m = jnp.maximum(jnp.max(s, axis=-1, keepdims=True), sk)
        p = jnp.exp(s - m)
        denom = jnp.sum(p, axis=-1, keepdims=True) + jnp.exp(sk - m)
        outs.append(_dot(p.astype(BF16), vj) / denom)
    o_ref[...] = jnp.concatenate(outs, axis=-1).astype(BF16)


def _attention(sink, q, k, v, bias, bsz, seq):
    n_blocks = seq // WINDOW

    def bias_idx(b, n):
        return (jnp.where(n == 0, 0, jnp.where(n == n_blocks - 1, 2, 1)), 0, 0, 0)

    return pl.pallas_call(
        functools.partial(_attn_kernel, n_blocks=n_blocks),
        name="band_attn",
        grid=(bsz, n_blocks),
        in_specs=[
            pl.BlockSpec(memory_space=pltpu.SMEM),
            pl.BlockSpec((WINDOW, ATT_W), lambda b, n: (b * n_blocks + n, 0)),
            pl.BlockSpec((1, seq, KV_W), lambda b, n: (b, 0, 0)),
            pl.BlockSpec((1, seq, KV_W), lambda b, n: (b, 0, 0)),
            pl.BlockSpec((1, N_HEADS, WINDOW, 3 * WINDOW), bias_idx),
        ],
        out_specs=pl.BlockSpec((WINDOW, ATT_W), lambda b, n: (b * n_blocks + n, 0)),
        out_shape=jax.ShapeDtypeStruct((bsz * seq, ATT_W), BF16),
        compiler_params=pltpu.CompilerParams(dimension_semantics=("arbitrary", "arbitrary"),
                                             vmem_limit_bytes=VMEM_LIMIT),
    )(sink, q, k.reshape(bsz, seq, KV_W), v.reshape(bsz, seq, KV_W), bias)


def _attn_bias(seq):
    q_idx = jnp.arange(WINDOW)[:, None]
    c_idx = jnp.arange(3 * WINDOW)[None, :]
    dist = jnp.abs(q_idx - c_idx + WINDOW)
    slopes = jnp.exp2(-8.0 * jnp.arange(1, N_HEADS + 1, dtype=F32) / N_HEADS)
    alibi = -slopes[:, None, None] * dist.astype(F32)[None]
    in_band = (dist <= WINDOW)[None]
    blk = (c_idx // WINDOW)[None]
    variants = []
    for bad in (0, -1, 2):
        ok = in_band & (blk != bad)
        variants.append(jnp.where(ok, alibi, -1e30))
    return jnp.stack(variants).astype(F32)


def _ssm_kernel(u_ref, t_ref, ms_ref, mo_ref, mult_ref, dsk_ref, y_ref, *, chunks_per_seq):
    nbc = u_ref.shape[0]
    slabs = [u_ref[:, h, :] for h in range(SSM_CH)]
    pieces = [jnp.concatenate([sl[:, i * SUB:(i + 1) * SUB] for sl in slabs], axis=1)
              for i in range(N_SUB)]
    ust = jnp.concatenate(pieces, axis=0)
    ub = ust.astype(BF16)
    z = _dot(ub, t_ref[0])
    s_all = _dot(ub, ms_ref[0])

    def mrow(d, r):
        return mult_ref[0, 2 * d, r:r + 1, :], mult_ref[0, 2 * d + 1, r:r + 1, :]

    def cmul(x, m):
        return x * m[0] + pltpu.roll(x, SSM_P, 1) * m[1]

    sf = [s_all[i * nbc:(i + 1) * nbc, :LANES] for i in range(N_SUB)]
    sb = [s_all[i * nbc:(i + 1) * nbc, LANES:] for i in range(N_SUB)]
    a32f, a32b = mrow(0, 0), mrow(1, 0)
    zero = jnp.zeros((nbc, LANES), F32)
    pf = [zero]
    for i in range(N_SUB - 1):
        pf.append(cmul(pf[i], a32f) + sf[i])
    totf = cmul(pf[N_SUB - 1], a32f) + sf[N_SUB - 1]
    pb = [zero] * N_SUB
    for i in range(N_SUB - 1, 0, -1):
        pb[i - 1] = cmul(pb[i], a32b) + sb[i]
    totb = cmul(pb[0], a32b) + sb[0]

    c_idx = lax.broadcasted_iota(jnp.int32, (nbc, LANES), 0) & (chunks_per_seq - 1)
    xf, xb = totf, totb
    for kk in range(_N_SCAN_STEPS):
        dd = 1 << kk
        shf = jnp.where(c_idx >= dd, pltpu.roll(xf, dd, 0), 0.0)
        xf = xf + cmul(shf, mrow(0, 3 + kk))
        shb = jnp.where(c_idx < chunks_per_seq - dd, pltpu.roll(xb, nbc - dd, 0), 0.0)
        xb = xb + cmul(shb, mrow(1, 3 + kk))
    ef = jnp.where(c_idx >= 1, pltpu.roll(xf, 1, 0), 0.0)
    eb = jnp.where(c_idx < chunks_per_seq - 1, pltpu.roll(xb, nbc - 1, 0), 0.0)

    xin = []
    for i in range(N_SUB):
        xf_i = pf[i] + (cmul(ef, mrow(0, i - 1)) if i > 0 else ef)
        jb = N_SUB - 1 - i
        xb_i = pb[i] + (cmul(eb, mrow(1, jb - 1)) if jb > 0 else eb)
        xin.append(jnp.concatenate([xf_i, xb_i], axis=1))
    xin = jnp.concatenate(xin, axis=0).astype(BF16)

    y = z + _dot(xin, mo_ref[0]) + dsk_ref[0] * ust
    g = jax.nn.gelu(y)
    for h in range(SSM_CH):
        y_ref[:, h, :] = jnp.concatenate(
            [g[i * nbc:(i + 1) * nbc, h * SUB:(h + 1) * SUB] for i in range(N_SUB)], axis=1)


def _ssm(u3, t_all, ms_all, mo_all, mult_all, dsk_all, layer, chunks_per_seq):
    nbc = u3.shape[0]
    base = layer * SSM_G
    return pl.pallas_call(
        functools.partial(_ssm_kernel, chunks_per_seq=chunks_per_seq),
        name="ssm_mix",
        grid=(SSM_G,),
        in_specs=[
            pl.BlockSpec((nbc, SSM_CH, LANES), lambda g: (0, g, 0)),
            pl.BlockSpec((1, CW, CW), lambda g: (base + g, 0, 0)),
            pl.BlockSpec((1, CW, 4 * SSM_P), lambda g: (base + g, 0, 0)),
            pl.BlockSpec((1, 4 * SSM_P, CW), lambda g: (base + g, 0, 0)),
            pl.BlockSpec((1, 4, _POW_ROWS, LANES), lambda g: (base + g, 0, 0, 0)),
            pl.BlockSpec((1, 1, CW), lambda g: (base + g, 0, 0)),
        ],
        out_specs=pl.BlockSpec((nbc, SSM_CH, LANES), lambda g: (0, g, 0)),
        out_shape=jax.ShapeDtypeStruct((nbc, SSM_W, LANES), F32),
        compiler_params=pltpu.CompilerParams(dimension_semantics=("arbitrary",),
                                             vmem_limit_bytes=VMEM_LIMIT),
    )(u3, t_all, ms_all, mo_all, mult_all, dsk_all)


def _outffn_kernel(x_ref, att_ref, y_ref, wglu_ref, wout_ref, g2_ref, w1_ref, w2_ref, o_ref):
    ys = [y_ref[j].T for j in range(TM_FFN // LANES)]
    y = jnp.concatenate(ys, axis=0).astype(BF16)
    gl = _dot(y, wglu_ref[...])
    ssm = gl[:, :SSM_W] * jax.nn.sigmoid(gl[:, SSM_W:])
    x = x_ref[...] + _dot(att_ref[...], wout_ref[:ATT_W, :]) \
        + _dot(ssm.astype(BF16), wout_ref[ATT_W:, :])
    ms = jnp.mean(x * x, axis=-1, keepdims=True)
    hn = (x * lax.rsqrt(ms + EPS) * g2_ref[...]).astype(BF16)
    hid = jnp.maximum(_dot(hn, w1_ref[...]), 0.0)
    o_ref[...] = x + _dot((hid * hid).astype(BF16), w2_ref[...])


def _outffn(x, att, y3, wglu, wout, g2, w1, w2):
    t = x.shape[0]
    const = lambda i: (0, 0)
    single = pl.Buffered(1)
    return pl.pallas_call(
        _outffn_kernel,
        name="out_ffn",
        grid=(t // TM_FFN,),
        in_specs=[
            pl.BlockSpec((TM_FFN, D_MODEL), lambda i: (i, 0)),
            pl.BlockSpec((TM_FFN, ATT_W), lambda i: (i, 0)),
            pl.BlockSpec((TM_FFN // LANES, SSM_W, LANES), lambda i: (i, 0, 0)),
            pl.BlockSpec((SSM_W, 2 * SSM_W), const, pipeline_mode=single),
            pl.BlockSpec((D_MODEL, D_MODEL), const, pipeline_mode=single),
            pl.BlockSpec((1, D_MODEL), const),
            pl.BlockSpec((D_MODEL, D_FF), const, pipeline_mode=single),
            pl.BlockSpec((D_FF, D_MODEL), const, pipeline_mode=single),
        ],
        out_specs=pl.BlockSpec((TM_FFN, D_MODEL), lambda i: (i, 0)),
        out_shape=jax.ShapeDtypeStruct((t, D_MODEL), F32),
        compiler_params=pltpu.CompilerParams(dimension_semantics=("arbitrary",),
                                             vmem_limit_bytes=VMEM_LIMIT),
    )(x, att, y3, wglu, wout, g2, w1, w2)


def _block_diag_mean(width):
    idx = jnp.arange(width) // HEAD_DIM
    return jnp.where(idx[:, None] == idx[None, :], 1.0 / HEAD_DIM, 0.0).astype(BF16)


def kernel(x, norm1, w_in, q_gain, k_gain, sink, lam_re, lam_im, log_dt, b_re, b_im, c_re, c_im,
           d_skip, w_glu, w_out, norm2, w_ff1, w_ff2):
    bsz, seq, _ = x.shape
    depth = norm1.shape[0]
    tokens = bsz * seq
    chunks_per_seq = seq // LANES
    assert seq % LANES == 0 and tokens % TM_IN == 0 and tokens % TM_FFN == 0
    assert chunks_per_seq == 1 << _N_SCAN_STEPS and max(_POW_N) == LANES * chunks_per_seq // 2
    lg = depth * SSM_G

    ldt = jnp.broadcast_to(log_dt[..., None], lam_re.shape)
    lam = jnp.stack([lam_re[:, 0], lam_im[:, 0], ldt[:, 0], lam_re[:, 1], lam_im[:, 1], ldt[:, 1]],
                    axis=2).reshape(lg, 6, SSM_P)
    bt = jnp.stack([b_re, b_im], axis=2).transpose(0, 1, 2, 4, 3).reshape(lg, 2, SSM_CH, SSM_P)
    cc = jnp.stack([c_re[:, 0], c_im[:, 0], c_re[:, 1], c_im[:, 1]], axis=2)
    cc = cc.reshape(lg, 4, SSM_CH, SSM_P)
    nrow = jnp.zeros((_POW_ROWS,), F32).at[:len(_POW_N)].set(jnp.array(_POW_N, F32))
    nrow = jnp.broadcast_to(nrow[:, None], (_POW_ROWS, SSM_P))

    t_all, ms4, mo4, pw = _ssm_prep(lam, bt, cc, nrow)
    ms_all = jnp.concatenate([ms4[:, k] for k in range(4)], axis=-1).astype(BF16)
    mo_all = jnp.concatenate([mo4[:, k].transpose(0, 2, 1) for k in range(4)], axis=1).astype(BF16)
    mult_all = jnp.stack([
        jnp.concatenate([pw[:, 0], pw[:, 0]], axis=-1), jnp.concatenate([pw[:, 1], pw[:, 2]], axis=-1),
        jnp.concatenate([pw[:, 3], pw[:, 3]], axis=-1), jnp.concatenate([pw[:, 4], pw[:, 5]], axis=-1),
    ], axis=1)
    dsk_all = jnp.repeat(d_skip.reshape(lg, 1, SSM_CH), SUB, axis=-1)

    bias = _attn_bias(seq)
    bdq = _block_diag_mean(ATT_W)
    bdk = _block_diag_mean(KV_W)

    xt = x.reshape(tokens, D_MODEL)
    for layer in range(depth):
        wqkv = w_in[layer, :, :QKV_W].astype(BF16)
        wut = w_in[layer, :, QKV_W:].T.astype(BF16)
        qg = (jnp.tile(q_gain[layer], N_HEADS) * (1.0 / math.sqrt(HEAD_DIM)))[None]
        kg = jnp.tile(k_gain[layer], N_KV)[None]
        q, k, v, u3 = _inproj(xt, norm1[layer][None], wqkv, wut, qg, kg, bdq, bdk)
        att = _attention(sink[layer], q, k, v, bias, bsz, seq)
        y3 = _ssm(u3, t_all, ms_all, mo_all, mult_all, dsk_all, layer, chunks_per_seq)
        xt = _outffn(xt, att, y3, w_glu[layer].astype(BF16), w_out[layer].astype(BF16),
                     norm2[layer][None], w_ff1[layer].astype(BF16), w_ff2[layer].astype(BF16))
    return xt.reshape(bsz, seq, D_MODEL)
```

```python
import functools
import math

import jax
import jax.numpy as jnp
from jax import lax
from jax.experimental import pallas as pl
from jax.experimental.pallas import tpu as pltpu

D_MODEL = 1024
N_HEADS = 8
N_KV = 2
GQA = N_HEADS // N_KV
HEAD_DIM = 64
ATT_W = N_HEADS * HEAD_DIM
KV_W = N_KV * HEAD_DIM
QKV_W = ATT_W + 2 * KV_W
WINDOW = 128
SSM_W = 512
SSM_CH = 16
SSM_G = SSM_W // SSM_CH
SSM_P = 64
D_FF = 4 * D_MODEL
EPS = 1e-6

LANES = 128
SUB = 32
N_SUB = LANES // SUB
CW = SSM_CH * SUB
VMEM_LIMIT = 56 * 1024 * 1024

TM_IN = 512
TM_FFN = 512
ATT_NB = 4

_POW_N = (32, 64, 96, 128, 256, 512, 1024, 2048, 4096)
_POW_ROWS = 16
_N_SCAN_STEPS = 6

F32 = jnp.float32
BF16 = jnp.bfloat16
I32 = jnp.int32


def _dot(a, b):
    return jnp.dot(a, b, preferred_element_type=F32)


def _dot_nt(a, b, precision=None):
    return lax.dot_general(a, b, (((1,), (1,)), ((), ())), precision=precision,
                           preferred_element_type=F32)


def _prep_kernel(lam_ref, bt_ref, c_ref, nrow_ref, t_ref, ms_ref, mo_ref, pw_ref,
                 g1r_ref, g1i_ref, mtr_ref, mti_ref):
    s_col = lax.broadcasted_iota(I32, (SUB, LANES), 0).astype(F32)
    is_fwd = lax.broadcasted_iota(I32, (SUB, LANES), 1) < SSM_P

    lr = lam_ref[0, 0:1, :]
    li = lam_ref[0, 1:2, :]
    dt = jnp.exp(lam_ref[0, 2:3, :])
    lrdt = lr * dt
    lidt = li * dt

    def powers(nmat):
        mag = jnp.exp(nmat * lrdt)
        ang = nmat * lidt
        return mag * jnp.cos(ang), mag * jnp.sin(ang)

    mag = jnp.exp(lrdt)
    abr = mag * jnp.cos(lidt)
    abi = mag * jnp.sin(lidt)
    den = lr * lr + li * li
    zr = ((abr - 1.0) * lr + abi * li) / den
    zi = (abi * lr - (abr - 1.0) * li) / den
    btr = bt_ref[0, 0]
    bti = bt_ref[0, 1]
    bbr = zr * btr - zi * bti
    bbi = zr * bti + zi * btr
    cr = c_ref[0, 0]
    ci = c_ref[0, 1]

    msr, msi = powers(jnp.where(is_fwd, 31.0 - s_col, s_col))
    tr, ti = powers(jnp.where(is_fwd, s_col, 31.0 - s_col))
    mor, moi = powers(jnp.where(is_fwd, s_col + 1.0, 32.0 - s_col))

    for h in range(SSM_CH):
        rows = slice(h * SUB, (h + 1) * SUB)
        b_r = bbr[h:h + 1, :]
        b_i = bbi[h:h + 1, :]
        ms_ref[0, rows, :LANES] = (msr * b_r - msi * b_i).astype(BF16)
        ms_ref[0, rows, LANES:] = (msr * b_i + msi * b_r).astype(BF16)
        c_r = cr[h:h + 1, :]
        c_i = ci[h:h + 1, :]
        g1r_ref[rows, :] = tr * c_r - ti * c_i
        g1i_ref[rows, :] = tr * c_i + ti * c_r
        mtr_ref[rows, :] = mor * c_r - moi * c_i
        mti_ref[rows, :] = -(mor * c_i + moi * c_r)
    mo_ref[0, :LANES, :] = mtr_ref[...].T.astype(BF16)
    mo_ref[0, LANES:, :] = mti_ref[...].T.astype(BF16)

    fwd16 = lax.broadcasted_iota(I32, (SSM_CH, LANES), 1) < SSM_P
    hp = lax.Precision.HIGHEST
    g1r = g1r_ref[...]
    g1i = g1i_ref[...]
    kts = []
    for sel in (fwd16, jnp.logical_not(fwd16)):
        kts.append(_dot_nt(jnp.where(sel, bbr, 0.0), g1r, precision=hp)
                   - _dot_nt(jnp.where(sel, bbi, 0.0), g1i, precision=hp))

    pr, pi = powers(nrow_ref[...])
    pw_ref[0, 0] = pr
    pw_ref[0, 1] = pi

    s_lane = lax.broadcasted_iota(I32, (SUB, CW), 1) & (SUB - 1)
    s_row = lax.broadcasted_iota(I32, (SUB, CW), 0)
    fwd_mask = s_lane >= s_row
    bwd_mask = s_lane <= s_row
    for h in range(SSM_CH):
        kf = jnp.broadcast_to(kts[0][h:h + 1, :], (SUB, CW))
        kb = jnp.broadcast_to(kts[1][h:h + 1, :], (SUB, CW))
        rf = pltpu.roll(kf, 0, 1, stride=1, stride_axis=0)
        rb = pltpu.roll(kb, CW - (SUB - 1), 1, stride=1, stride_axis=0)
        blk = jnp.where(fwd_mask, rf, 0.0) + jnp.where(bwd_mask, rb, 0.0)
        t_ref[0, h * SUB:(h + 1) * SUB, :] = blk.astype(BF16)


def _ssm_prep(lam, bt, c, nrow):
    lg = lam.shape[0]
    return pl.pallas_call(
        _prep_kernel,
        name="ssm_prep",
        grid=(lg,),
        in_specs=[
            pl.BlockSpec((1, 3, LANES), lambda i: (i, 0, 0)),
            pl.BlockSpec((1, 2, SSM_CH, LANES), lambda i: (i, 0, 0, 0)),
            pl.BlockSpec((1, 2, SSM_CH, LANES), lambda i: (i, 0, 0, 0)),
            pl.BlockSpec((_POW_ROWS, LANES), lambda i: (0, 0)),
        ],
        out_specs=[
            pl.BlockSpec((1, CW, CW), lambda i: (i, 0, 0)),
            pl.BlockSpec((1, CW, 2 * LANES), lambda i: (i, 0, 0)),
            pl.BlockSpec((1, 2 * LANES, CW), lambda i: (i, 0, 0)),
            pl.BlockSpec((1, 2, _POW_ROWS, LANES), lambda i: (i, 0, 0, 0)),
        ],
        out_shape=[
            jax.ShapeDtypeStruct((lg, CW, CW), BF16),
            jax.ShapeDtypeStruct((lg, CW, 2 * LANES), BF16),
            jax.ShapeDtypeStruct((lg, 2 * LANES, CW), BF16),
            jax.ShapeDtypeStruct((lg, 2, _POW_ROWS, LANES), F32),
        ],
        scratch_shapes=[pltpu.VMEM((CW, LANES), F32)] * 4,
        compiler_params=pltpu.CompilerParams(dimension_semantics=("arbitrary",)),
    )(lam, bt, c, nrow)


def _inproj_kernel(x_ref, g1_ref, wqkv_ref, wut_ref, qg_ref, kg_ref, bdq_ref, bdk_ref,
                   q_ref, k_ref, v_ref, u_ref):
    x = x_ref[...]
    ms = jnp.mean(x * x, axis=-1, keepdims=True)
    hn = (x * lax.rsqrt(ms + EPS) * g1_ref[...]).astype(BF16)
    z = _dot(hn, wqkv_ref[...])

    def head_norm(t, bd_ref, gain):
        t2 = t * t
        hi = t2.astype(BF16)
        lo = (t2 - hi.astype(F32)).astype(BF16)
        m = _dot(hi, bd_ref[...]) + _dot(lo, bd_ref[...])
        return t * lax.rsqrt(m + EPS) * gain

    q_ref[...] = head_norm(z[:, :ATT_W], bdq_ref, qg_ref[...]).astype(BF16)
    k_ref[...] = head_norm(z[:, ATT_W:ATT_W + KV_W], bdk_ref, kg_ref[...]).astype(BF16)
    v_ref[...] = z[:, ATT_W + KV_W:].astype(BF16)
    ut = _dot_nt(wut_ref[...], hn)
    for j in range(TM_IN // LANES):
        for g in range(SSM_G):
            u_ref[g, j * SSM_CH:(j + 1) * SSM_CH, :] = ut[g * SSM_CH:(g + 1) * SSM_CH,
                                                          j * LANES:(j + 1) * LANES]


def _inproj(x, g1, wqkv, wut, qg, kg, bdq, bdk, layer):
    t = x.shape[0]
    const = lambda i: (0, 0)
    sel = lambda i: (layer, 0, 0)
    return pl.pallas_call(
        _inproj_kernel,
        name="inproj",
        grid=(t // TM_IN,),
        in_specs=[
            pl.BlockSpec((TM_IN, D_MODEL), lambda i: (i, 0)),
            pl.BlockSpec((None, 1, D_MODEL), sel),
            pl.BlockSpec((None, D_MODEL, QKV_W), sel),
            pl.BlockSpec((None, SSM_W, D_MODEL), sel),
            pl.BlockSpec((None, 1, ATT_W), sel),
            pl.BlockSpec((None, 1, KV_W), sel),
            pl.BlockSpec((ATT_W, ATT_W), const),
            pl.BlockSpec((KV_W, KV_W), const),
        ],
        out_specs=[
            pl.BlockSpec((TM_IN, ATT_W), lambda i: (i, 0)),
            pl.BlockSpec((TM_IN, KV_W), lambda i: (i, 0)),
            pl.BlockSpec((TM_IN, KV_W), lambda i: (i, 0)),
            pl.BlockSpec((SSM_G, TM_IN // LANES * SSM_CH, LANES), lambda i: (0, i, 0)),
        ],
        out_shape=[
            jax.ShapeDtypeStruct((t, ATT_W), BF16),
            jax.ShapeDtypeStruct((t, KV_W), BF16),
            jax.ShapeDtypeStruct((t, KV_W), BF16),
            jax.ShapeDtypeStruct((SSM_G, t // LANES * SSM_CH, LANES), F32),
        ],
        compiler_params=pltpu.CompilerParams(dimension_semantics=("arbitrary",),
                                             vmem_limit_bytes=VMEM_LIMIT),
    )(x, g1, wqkv, wut, qg, kg, bdq, bdk)


def _attn_kernel(sink_ref, q_ref, k_ref, v_ref, bias_ref, o_ref, s_scr, p_scr, *, n_blocks):
    n0 = pl.program_id(1) * ATT_NB
    prev = pl.multiple_of(jnp.maximum(n0 - 1, 0) * WINDOW, WINDOW)
    mid = pl.multiple_of(n0 * WINDOW, WINDOW)
    nxt = pl.multiple_of(jnp.minimum(n0 + ATT_NB, n_blocks - 1) * WINDOW, WINDOW)

    def window(ref):
        return jnp.concatenate([ref[0, pl.ds(prev, WINDOW), :], ref[0, pl.ds(mid, ATT_NB * WINDOW), :],
                                ref[0, pl.ds(nxt, WINDOW), :]], axis=0)

    kw = window(k_ref)
    vw = window(v_ref)
    kj = [kw[:, j * HEAD_DIM:(j + 1) * HEAD_DIM] for j in range(N_KV)]
    vj = [vw[:, j * HEAD_DIM:(j + 1) * HEAD_DIM] for j in range(N_KV)]

    def scores(b):
        n = n0 + b
        var = jnp.where(n == 0, 0, jnp.where(n == n_blocks - 1, 2, 1))
        qb = q_ref[b * WINDOW:(b + 1) * WINDOW, :]
        for h in range(N_HEADS):
            kh = kj[h // GQA][b * WINDOW:(b + 3) * WINDOW, :]
            s_scr[b % 2, h] = _dot_nt(qb[:, h * HEAD_DIM:(h + 1) * HEAD_DIM], kh) + bias_ref[var, h]

    def softmax(b):
        inv = []
        for h in range(N_HEADS):
            s = s_scr[b % 2, h]
            sk = sink_ref[h]
            m = jnp.maximum(jnp.max(s, axis=-1, keepdims=True), sk)
            p = jnp.exp(s - m)
            inv.append(1.0 / (jnp.sum(p, axis=-1, keepdims=True) + jnp.exp(sk - m)))
            p_scr[b % 2, h] = p.astype(BF16)
        return inv

    def weighted(b, inv):
        outs = []
        for h in range(N_HEADS):
            vh = vj[h // GQA][b * WINDOW:(b + 3) * WINDOW, :]
            outs.append(_dot(p_scr[b % 2, h], vh) * inv[h])
        o_ref[b * WINDOW:(b + 1) * WINDOW, :] = jnp.concatenate(outs, axis=-1).astype(BF16)

    scores(0)
    for b in range(ATT_NB):
        if b + 1 < ATT_NB:
            scores(b + 1)
        weighted(b, softmax(b))


def _attention(sink, q, k, v, bias, bsz, seq, layer):
    n_blocks = seq // WINDOW
    steps = n_blocks // ATT_NB
    rows = ATT_NB * WINDOW
    return pl.pallas_call(
        functools.partial(_attn_kernel, n_blocks=n_blocks),
        name="band_attn",
        grid=(bsz, steps),
        in_specs=[
            pl.BlockSpec(memory_space=pltpu.SMEM),
            pl.BlockSpec((rows, ATT_W), lambda b, n: (b * steps + n, 0)),
            pl.BlockSpec((1, seq, KV_W), lambda b, n: (b, 0, 0)),
            pl.BlockSpec((1, seq, KV_W), lambda b, n: (b, 0, 0)),
            pl.BlockSpec((3, N_HEADS, WINDOW, 3 * WINDOW), lambda b, n: (0, 0, 0, 0)),
        ],
        out_specs=pl.BlockSpec((rows, ATT_W), lambda b, n: (b * steps + n, 0)),
        out_shape=jax.ShapeDtypeStruct((bsz * seq, ATT_W), BF16),
        scratch_shapes=[pltpu.VMEM((2, N_HEADS, WINDOW, 3 * WINDOW), F32),
                        pltpu.VMEM((2, N_HEADS, WINDOW, 3 * WINDOW), BF16)],
        compiler_params=pltpu.CompilerParams(dimension_semantics=("arbitrary", "arbitrary"),
                                             vmem_limit_bytes=VMEM_LIMIT),
    )(sink[layer], q, k.reshape(bsz, seq, KV_W), v.reshape(bsz, seq, KV_W), bias)


def _attn_bias(seq):
    q_idx = jnp.arange(WINDOW)[:, None]
    c_idx = jnp.arange(3 * WINDOW)[None, :]
    dist = jnp.abs(q_idx - c_idx + WINDOW)
    slopes = jnp.exp2(-8.0 * jnp.arange(1, N_HEADS + 1, dtype=F32) / N_HEADS)
    alibi = -slopes[:, None, None] * dist.astype(F32)[None]
    in_band = (dist <= WINDOW)[None]
    blk = (c_idx // WINDOW)[None]
    variants = []
    for bad in (0, -1, 2):
        ok = in_band & (blk != bad)
        variants.append(jnp.where(ok, alibi, -1e30))
    return jnp.stack(variants).astype(F32)


def _swap_lane_blocks(vs):
    lane = lax.broadcasted_iota(I32, vs[0].shape, 1)
    lo_half = lane < 2 * SUB
    even_blk = (lane & SUB) == 0
    a0, a1, a2, a3 = vs
    c0 = jnp.where(lo_half, a0, pltpu.roll(a2, 2 * SUB, 1))
    c2 = jnp.where(lo_half, pltpu.roll(a0, 2 * SUB, 1), a2)
    c1 = jnp.where(lo_half, a1, pltpu.roll(a3, 2 * SUB, 1))
    c3 = jnp.where(lo_half, pltpu.roll(a1, 2 * SUB, 1), a3)
    b0 = jnp.where(even_blk, c0, pltpu.roll(c1, SUB, 1))
    b1 = jnp.where(even_blk, pltpu.roll(c0, LANES - SUB, 1), c1)
    b2 = jnp.where(even_blk, c2, pltpu.roll(c3, SUB, 1))
    b3 = jnp.where(even_blk, pltpu.roll(c2, LANES - SUB, 1), c3)
    return [b0, b1, b2, b3]


def _ssm_kernel(u_ref, t_ref, ms_ref, mo_ref, pw_ref, dsk_ref, y_ref, *, chunks_per_seq):
    nbc = u_ref.shape[0] // SSM_CH
    slabs = [u_ref[pl.ds(h, nbc, stride=SSM_CH), :] for h in range(SSM_CH)]
    cols = [_swap_lane_blocks(slabs[4 * q:4 * q + 4]) for q in range(SSM_CH // 4)]
    pieces = [jnp.concatenate([cols[q][i] for q in range(SSM_CH // 4)], axis=1)
              for i in range(N_SUB)]
    ust = jnp.concatenate(pieces, axis=0)
    ub = ust.astype(BF16)
    z = _dot(ub, t_ref[0])
    s_all = _dot(ub, ms_ref[0])

    is_fwd = lax.broadcasted_iota(I32, (nbc, LANES), 1) < SSM_P
    c_idx = lax.broadcasted_iota(I32, (nbc, LANES), 0) & (chunks_per_seq - 1)

    def prow(r):
        return pw_ref[0, 0, r:r + 1, :], pw_ref[0, 1, r:r + 1, :]

    def cmul(x, a):
        return a[0] * x[0] - a[1] * x[1], a[0] * x[1] + a[1] * x[0]

    def cadd(x, y):
        return x[0] + y[0], x[1] + y[1]

    def mirror(k):
        lo, hi = k * nbc, (N_SUB - 1 - k) * nbc
        return tuple(jnp.where(is_fwd, s_all[lo:lo + nbc, c:c + LANES], s_all[hi:hi + nbc, c:c + LANES])
                     for c in (0, LANES))

    a32 = prow(0)
    sk = [mirror(k) for k in range(N_SUB)]
    zero = jnp.zeros((nbc, LANES), F32)
    pk = [(zero, zero)]
    for k in range(N_SUB - 1):
        pk.append(cadd(cmul(pk[k], a32), sk[k]))
    tot = cadd(cmul(pk[N_SUB - 1], a32), sk[N_SUB - 1])

    def shifted(x, dd):
        ok_f = jnp.logical_and(is_fwd, c_idx >= dd)
        ok_b = jnp.logical_and(jnp.logical_not(is_fwd), c_idx < chunks_per_seq - dd)
        return tuple(jnp.where(ok_f, pltpu.roll(t, dd, 0), jnp.where(ok_b, pltpu.roll(t, nbc - dd, 0), 0.0))
                     for t in x)

    x = tot
    for kk in range(_N_SCAN_STEPS):
        x = cadd(x, cmul(shifted(x, 1 << kk), prow(3 + kk)))
    ent = shifted(x, 1)

    qk = [cadd(pk[k], cmul(ent, prow(k - 1)) if k > 0 else ent) for k in range(N_SUB)]
    xin = []
    for i in range(N_SUB):
        j = N_SUB - 1 - i
        xin.append(jnp.concatenate([jnp.where(is_fwd, qk[i][c], qk[j][c]) for c in (0, 1)], axis=1))
    xin = jnp.concatenate(xin, axis=0).astype(BF16)

    y = z + _dot(xin, mo_ref[0]) + dsk_ref[0] * ust
    g = jax.nn.gelu(y)
    for q in range(SSM_CH // 4):
        back = _swap_lane_blocks([g[i * nbc:(i + 1) * nbc, q * LANES:(q + 1) * LANES]
                                  for i in range(N_SUB)])
        for r in range(4):
            y_ref[pl.ds(4 * q + r, nbc, stride=SSM_CH), :] = back[r]


def _ssm(u3, t_all, ms_all, mo_all, pw_all, dsk_all, layer, chunks_per_seq):
    rows = u3.shape[1]
    base = layer * SSM_G
    return pl.pallas_call(
        functools.partial(_ssm_kernel, chunks_per_seq=chunks_per_seq),
        name="ssm_mix",
        grid=(SSM_G,),
        in_specs=[
            pl.BlockSpec((None, rows, LANES), lambda g: (g, 0, 0)),
            pl.BlockSpec((1, CW, CW), lambda g: (base + g, 0, 0)),
            pl.BlockSpec((1, CW, 2 * LANES), lambda g: (base + g, 0, 0)),
            pl.BlockSpec((1, 2 * LANES, CW), lambda g: (base + g, 0, 0)),
            pl.BlockSpec((1, 2, _POW_ROWS, LANES), lambda g: (base + g, 0, 0, 0)),
            pl.BlockSpec((1, 1, CW), lambda g: (base + g, 0, 0)),
        ],
        out_specs=pl.BlockSpec((None, rows, LANES), lambda g: (g, 0, 0)),
        out_shape=jax.ShapeDtypeStruct((SSM_G, rows, LANES), F32),
        compiler_params=pltpu.CompilerParams(dimension_semantics=("arbitrary",),
                                             vmem_limit_bytes=VMEM_LIMIT),
    )(u3, t_all, ms_all, mo_all, pw_all, dsk_all)


def _outffn_kernel(x_ref, att_ref, y_ref, wglu_ref, wout_ref, g2_ref, w1_ref, w2_ref, o_ref):
    ys = [jnp.concatenate([y_ref[g, j * SSM_CH:(j + 1) * SSM_CH, :] for g in range(SSM_G)], axis=0).T
          for j in range(TM_FFN // LANES)]
    y = jnp.concatenate(ys, axis=0).astype(BF16)
    gl = _dot(y, wglu_ref[...])
    ssm = gl[:, :SSM_W] * jax.nn.sigmoid(gl[:, SSM_W:])
    x = x_ref[...] + _dot(att_ref[...], wout_ref[:ATT_W, :]) \
        + _dot(ssm.astype(BF16), wout_ref[ATT_W:, :])
    ms = jnp.mean(x * x, axis=-1, keepdims=True)
    hn = (x * lax.rsqrt(ms + EPS) * g2_ref[...]).astype(BF16)
    hid = jnp.maximum(_dot(hn, w1_ref[...]), 0.0)
    o_ref[...] = x + _dot((hid * hid).astype(BF16), w2_ref[...])


def _outffn(x, att, y3, wglu, wout, g2, w1, w2, layer):
    t = x.shape[0]
    sel = lambda i: (layer, 0, 0)
    single = pl.Buffered(1)
    return pl.pallas_call(
        _outffn_kernel,
        name="out_ffn",
        grid=(t // TM_FFN,),
        in_specs=[
            pl.BlockSpec((TM_FFN, D_MODEL), lambda i: (i, 0)),
            pl.BlockSpec((TM_FFN, ATT_W), lambda i: (i, 0)),
            pl.BlockSpec((SSM_G, TM_FFN // LANES * SSM_CH, LANES), lambda i: (0, i, 0)),
            pl.BlockSpec((None, SSM_W, 2 * SSM_W), sel, pipeline_mode=single),
            pl.BlockSpec((None, D_MODEL, D_MODEL), sel, pipeline_mode=single),
            pl.BlockSpec((None, 1, D_MODEL), sel),
            pl.BlockSpec((None, D_MODEL, D_FF), sel, pipeline_mode=single),
            pl.BlockSpec((None, D_FF, D_MODEL), sel, pipeline_mode=single),
        ],
        out_specs=pl.BlockSpec((TM_FFN, D_MODEL), lambda i: (i, 0)),
        out_shape=jax.ShapeDtypeStruct((t, D_MODEL), F32),
        compiler_params=pltpu.CompilerParams(dimension_semantics=("arbitrary",),
                                             vmem_limit_bytes=VMEM_LIMIT),
    )(x, att, y3, wglu, wout, g2, w1, w2)


def _block_diag_mean(width):
    idx = jnp.arange(width) // HEAD_DIM
    return jnp.where(idx[:, None] == idx[None, :], 1.0 / HEAD_DIM, 0.0).astype(BF16)


def _prep_operands(lam_re, lam_im, log_dt, b_re, b_im, c_re, c_im, d_skip):
    depth = lam_re.shape[0]
    lg = depth * SSM_G
    both = lambda a: jnp.concatenate([a[:, 0], a[:, 1]], axis=-1)
    ldt = jnp.broadcast_to(log_dt[..., None], lam_re.shape)
    lam = jnp.stack([both(lam_re), both(lam_im), both(ldt)], axis=2).reshape(lg, 3, LANES)
    bt = jnp.stack([b_re, b_im], axis=2).transpose(0, 1, 2, 4, 3)
    bt = jnp.concatenate([bt, bt], axis=-1).reshape(lg, 2, SSM_CH, LANES)
    cc = jnp.stack([both(c_re), both(c_im)], axis=2).reshape(lg, 2, SSM_CH, LANES)
    nrow = jnp.zeros((_POW_ROWS,), F32).at[:len(_POW_N)].set(jnp.array(_POW_N, F32))
    nrow = jnp.broadcast_to(nrow[:, None], (_POW_ROWS, LANES))
    dsk = jnp.repeat(d_skip.reshape(lg, 1, SSM_CH), SUB, axis=-1)
    return list(_ssm_prep(lam, bt, cc, nrow)) + [dsk]


def kernel(x, norm1, w_in, q_gain, k_gain, sink, lam_re, lam_im, log_dt, b_re, b_im, c_re, c_im,
           d_skip, w_glu, w_out, norm2, w_ff1, w_ff2):
    bsz, seq, _ = x.shape
    depth = norm1.shape[0]
    tokens = bsz * seq
    chunks_per_seq = seq // LANES
    assert seq % (ATT_NB * WINDOW) == 0 and tokens % TM_IN == 0 and tokens % TM_FFN == 0
    assert chunks_per_seq == 1 << _N_SCAN_STEPS and max(_POW_N) == LANES * chunks_per_seq // 2

    t_all, ms_all, mo_all, pw_all, dsk_all = _prep_operands(lam_re, lam_im, log_dt, b_re, b_im,
                                                           c_re, c_im, d_skip)
    wqkv = w_in[:, :, :QKV_W].astype(BF16)
    wut = w_in[:, :, QKV_W:].transpose(0, 2, 1).astype(BF16)
    wglu = w_glu.astype(BF16)
    wout = w_out.astype(BF16)
    w1 = w_ff1.astype(BF16)
    w2 = w_ff2.astype(BF16)
    g1 = norm1[:, None, :]
    g2 = norm2[:, None, :]
    qg = (jnp.tile(q_gain, (1, N_HEADS)) * (1.0 / math.sqrt(HEAD_DIM)))[:, None, :]
    kg = jnp.tile(k_gain, (1, N_KV))[:, None, :]
    bias = _attn_bias(seq)
    bdq = _block_diag_mean(ATT_W)
    bdk = _block_diag_mean(KV_W)

    xt = x.reshape(tokens, D_MODEL)
    for layer in range(depth):
        q, k, v, u3 = _inproj(xt, g1, wqkv, wut, qg, kg, bdq, bdk, layer)
        att = _attention(sink, q, k, v, bias, bsz, seq, layer)
        y3 = _ssm(u3, t_all, ms_all, mo_all, pw_all, dsk_all, layer, chunks_per_seq)
        xt = _outffn(xt, att, y3, wglu, wout, g2, w1, w2, layer)
    return xt.reshape(bsz, seq, D_MODEL)
```

```python
import functools
import math

import jax
import jax.numpy as jnp
from jax import lax
from jax.experimental import pallas as pl
from jax.experimental.pallas import tpu as pltpu

D_MODEL = 1024
N_HEADS = 8
N_KV = 2
GQA = N_HEADS // N_KV
HEAD_DIM = 64
ATT_W = N_HEADS * HEAD_DIM
KV_W = N_KV * HEAD_DIM
QKV_W = ATT_W + 2 * KV_W
WINDOW = 128
SSM_W = 512
SSM_CH = 16
SSM_G = SSM_W // SSM_CH
SSM_P = 64
D_FF = 4 * D_MODEL
EPS = 1e-6
LOG2E = math.log2(math.e)

LANES = 128
SUB = 32
N_SUB = LANES // SUB
CW = SSM_CH * SUB
VMEM_LIMIT = 56 * 1024 * 1024

TM_IN = 1024
TM_HALF = 512
TM_FFN = 512
Y_CHUNKS = 8
ATT_NB = 4
SSM_GB = 2

_POW_N = (32, 64, 96, 128, 256, 512, 1024, 2048, 4096)
_POW_ROWS = 16
_N_SCAN_STEPS = 6

F32 = jnp.float32
BF16 = jnp.bfloat16
I32 = jnp.int32


def _dot(a, b):
    return jnp.dot(a, b, preferred_element_type=F32)


def _dot_nt(a, b, precision=None):
    return lax.dot_general(a, b, (((1,), (1,)), ((), ())), precision=precision,
                           preferred_element_type=F32)


def _prep_kernel(lam_ref, bt_ref, c_ref, nrow_ref, t_ref, ms_ref, mo_ref, pw_ref,
                 g1r_ref, g1i_ref, mtr_ref, mti_ref):
    s_col = lax.broadcasted_iota(I32, (SUB, LANES), 0).astype(F32)
    is_fwd = lax.broadcasted_iota(I32, (SUB, LANES), 1) < SSM_P

    lr = lam_ref[0, 0:1, :]
    li = lam_ref[0, 1:2, :]
    dt = jnp.exp(lam_ref[0, 2:3, :])
    lrdt = lr * dt
    lidt = li * dt

    def powers(nmat):
        mag = jnp.exp(nmat * lrdt)
        ang = nmat * lidt
        return mag * jnp.cos(ang), mag * jnp.sin(ang)

    mag = jnp.exp(lrdt)
    abr = mag * jnp.cos(lidt)
    abi = mag * jnp.sin(lidt)
    den = lr * lr + li * li
    zr = ((abr - 1.0) * lr + abi * li) / den
    zi = (abi * lr - (abr - 1.0) * li) / den
    btr = bt_ref[0, 0]
    bti = bt_ref[0, 1]
    bbr = zr * btr - zi * bti
    bbi = zr * bti + zi * btr
    cr = c_ref[0, 0]
    ci = c_ref[0, 1]

    msr, msi = powers(jnp.where(is_fwd, 31.0 - s_col, s_col))
    tr, ti = powers(jnp.where(is_fwd, s_col, 31.0 - s_col))
    mor, moi = tr * abr - ti * abi, tr * abi + ti * abr

    for h in range(SSM_CH):
        rows = slice(h * SUB, (h + 1) * SUB)
        b_r = bbr[h:h + 1, :]
        b_i = bbi[h:h + 1, :]
        ms_ref[0, rows, :LANES] = (msr * b_r - msi * b_i).astype(BF16)
        ms_ref[0, rows, LANES:] = (msr * b_i + msi * b_r).astype(BF16)
        c_r = cr[h:h + 1, :]
        c_i = ci[h:h + 1, :]
        g1r_ref[rows, :] = tr * c_r - ti * c_i
        g1i_ref[rows, :] = tr * c_i + ti * c_r
        mtr_ref[rows, :] = mor * c_r - moi * c_i
        mti_ref[rows, :] = -(mor * c_i + moi * c_r)
    mo_ref[0, :LANES, :] = mtr_ref[...].T.astype(BF16)
    mo_ref[0, LANES:, :] = mti_ref[...].T.astype(BF16)

    fwd16 = lax.broadcasted_iota(I32, (SSM_CH, LANES), 1) < SSM_P
    hp = lax.Precision.HIGHEST
    g1r = g1r_ref[...]
    g1i = g1i_ref[...]
    kts = []
    for sel in (fwd16, jnp.logical_not(fwd16)):
        kts.append(_dot_nt(jnp.where(sel, bbr, 0.0), g1r, precision=hp)
                   - _dot_nt(jnp.where(sel, bbi, 0.0), g1i, precision=hp))

    pr, pi = powers(nrow_ref[...])
    pw_ref[0, 0] = pr
    pw_ref[0, 1] = pi

    s_lane = lax.broadcasted_iota(I32, (SUB, CW), 1) & (SUB - 1)
    s_row = lax.broadcasted_iota(I32, (SUB, CW), 0)
    fwd_mask = s_lane >= s_row
    bwd_mask = s_lane <= s_row
    for h in range(SSM_CH):
        kf = jnp.broadcast_to(kts[0][h:h + 1, :], (SUB, CW))
        kb = jnp.broadcast_to(kts[1][h:h + 1, :], (SUB, CW))
        rf = pltpu.roll(kf, 0, 1, stride=1, stride_axis=0)
        rb = pltpu.roll(kb, CW - (SUB - 1), 1, stride=1, stride_axis=0)
        blk = jnp.where(fwd_mask, rf, 0.0) + jnp.where(bwd_mask, rb, 0.0)
        t_ref[0, h * SUB:(h + 1) * SUB, :] = blk.astype(BF16)


def _ssm_prep(lam, bt, c, nrow):
    lg = lam.shape[0]
    return pl.pallas_call(
        _prep_kernel,
        name="ssm_prep",
        grid=(lg,),
        in_specs=[
            pl.BlockSpec((1, 3, LANES), lambda i: (i, 0, 0)),
            pl.BlockSpec((1, 2, SSM_CH, LANES), lambda i: (i, 0, 0, 0)),
            pl.BlockSpec((1, 2, SSM_CH, LANES), lambda i: (i, 0, 0, 0)),
            pl.BlockSpec((_POW_ROWS, LANES), lambda i: (0, 0)),
        ],
        out_specs=[
            pl.BlockSpec((1, CW, CW), lambda i: (i, 0, 0)),
            pl.BlockSpec((1, CW, 2 * LANES), lambda i: (i, 0, 0)),
            pl.BlockSpec((1, 2 * LANES, CW), lambda i: (i, 0, 0)),
            pl.BlockSpec((1, 2, _POW_ROWS, LANES), lambda i: (i, 0, 0, 0)),
        ],
        out_shape=[
            jax.ShapeDtypeStruct((lg, CW, CW), BF16),
            jax.ShapeDtypeStruct((lg, CW, 2 * LANES), BF16),
            jax.ShapeDtypeStruct((lg, 2 * LANES, CW), BF16),
            jax.ShapeDtypeStruct((lg, 2, _POW_ROWS, LANES), F32),
        ],
        scratch_shapes=[pltpu.VMEM((CW, LANES), F32)] * 4,
        compiler_params=pltpu.CompilerParams(dimension_semantics=("arbitrary",)),
    )(lam, bt, c, nrow)


def _inproj_kernel(x_ref, g1_ref, wqkv_ref, wut_ref, qg_ref, kg_ref, bdq_ref, bdk_ref,
                   q_ref, k_ref, v_ref, u_ref):
    def head_norm(t, bd_ref, gain):
        t2 = t * t
        hi = t2.astype(BF16)
        lo = (t2 - hi.astype(F32)).astype(BF16)
        m = _dot(hi, bd_ref[...]) + _dot(lo, bd_ref[...])
        return t * lax.rsqrt(m + EPS) * gain

    halves = [slice(r * TM_HALF, (r + 1) * TM_HALF) for r in range(TM_IN // TM_HALF)]
    hn = []
    for rows in halves:
        x = x_ref[rows, :]
        ms = jnp.mean(x * x, axis=-1, keepdims=True)
        hn.append((x * lax.rsqrt(ms + EPS) * g1_ref[...]).astype(BF16))
    z = [_dot(h, wqkv_ref[...]) for h in hn]
    ut = [_dot_nt(wut_ref[...], h) for h in hn]
    for rows, zz in zip(halves, z):
        q_ref[rows, :] = head_norm(zz[:, :ATT_W], bdq_ref, qg_ref[...]).astype(BF16)
        k_ref[rows, :] = head_norm(zz[:, ATT_W:ATT_W + KV_W], bdk_ref, kg_ref[...]).astype(BF16)
        v_ref[rows, :] = zz[:, ATT_W + KV_W:].astype(BF16)
    chunks = TM_IN // LANES
    u2d = u_ref.reshape(SSM_W * chunks, LANES)
    for r, utr in enumerate(ut):
        for jj in range(TM_HALF // LANES):
            j = r * (TM_HALF // LANES) + jj
            u2d[pl.ds(j, SSM_W, stride=chunks), :] = utr[:, jj * LANES:(jj + 1) * LANES]


def _inproj(x, g1, wqkv, wut, qg, kg, bdq, bdk, layer):
    t = x.shape[0]
    const = lambda i: (0, 0)
    sel = lambda i: (layer, 0, 0)
    return pl.pallas_call(
        _inproj_kernel,
        name="inproj",
        grid=(t // TM_IN,),
        in_specs=[
            pl.BlockSpec((TM_IN, D_MODEL), lambda i: (i, 0)),
            pl.BlockSpec((None, 1, D_MODEL), sel),
            pl.BlockSpec((None, D_MODEL, QKV_W), sel),
            pl.BlockSpec((None, SSM_W, D_MODEL), sel),
            pl.BlockSpec((None, 1, ATT_W), sel),
            pl.BlockSpec((None, 1, KV_W), sel),
            pl.BlockSpec((ATT_W, ATT_W), const),
            pl.BlockSpec((KV_W, KV_W), const),
        ],
        out_specs=[
            pl.BlockSpec((TM_IN, ATT_W), lambda i: (i, 0)),
            pl.BlockSpec((TM_IN, KV_W), lambda i: (i, 0)),
            pl.BlockSpec((TM_IN, KV_W), lambda i: (i, 0)),
            pl.BlockSpec((SSM_W, TM_IN // LANES, LANES), lambda i: (0, i, 0)),
        ],
        out_shape=[
            jax.ShapeDtypeStruct((t, ATT_W), BF16),
            jax.ShapeDtypeStruct((t, KV_W), BF16),
            jax.ShapeDtypeStruct((t, KV_W), BF16),
            jax.ShapeDtypeStruct((SSM_W, t // LANES, LANES), F32),
        ],
        compiler_params=pltpu.CompilerParams(dimension_semantics=("arbitrary",),
                                             vmem_limit_bytes=VMEM_LIMIT),
    )(x, g1, wqkv, wut, qg, kg, bdq, bdk)


def _attn_kernel(sink_ref, q_ref, k_ref, v_ref, bias_ref, o_ref, s_scr, p_scr, *, n_blocks):
    n0 = pl.program_id(1) * ATT_NB
    prev = pl.multiple_of(jnp.maximum(n0 - 1, 0) * WINDOW, WINDOW)
    mid = pl.multiple_of(n0 * WINDOW, WINDOW)
    nxt = pl.multiple_of(jnp.minimum(n0 + ATT_NB, n_blocks - 1) * WINDOW, WINDOW)

    def window(ref):
        return jnp.concatenate([ref[0, pl.ds(prev, WINDOW), :], ref[0, pl.ds(mid, ATT_NB * WINDOW), :],
                                ref[0, pl.ds(nxt, WINDOW), :]], axis=0)

    kw = window(k_ref)
    vw = window(v_ref)
    kj = [kw[:, j * HEAD_DIM:(j + 1) * HEAD_DIM] for j in range(N_KV)]
    vj = [vw[:, j * HEAD_DIM:(j + 1) * HEAD_DIM] for j in range(N_KV)]

    def scores(b):
        n = n0 + b
        var = jnp.where(n == 0, 0, jnp.where(n == n_blocks - 1, 2, 1))
        qb = q_ref[b * WINDOW:(b + 1) * WINDOW, :]
        for h in range(N_HEADS):
            kh = kj[h // GQA][b * WINDOW:(b + 3) * WINDOW, :]
            s_scr[b % 2, h] = _dot_nt(qb[:, h * HEAD_DIM:(h + 1) * HEAD_DIM], kh) + bias_ref[var, h]

    def softmax(b):
        inv = []
        for h in range(N_HEADS):
            s = s_scr[b % 2, h]
            sk = sink_ref[h] * LOG2E
            m = jnp.maximum(jnp.max(s, axis=-1, keepdims=True), sk)
            p = jnp.exp2(s - m)
            inv.append(1.0 / (jnp.sum(p, axis=-1, keepdims=True) + jnp.exp2(sk - m)))
            p_scr[b % 2, h] = p.astype(BF16)
        return inv

    def weighted(b, inv):
        outs = []
        for h in range(N_HEADS):
            vh = vj[h // GQA][b * WINDOW:(b + 3) * WINDOW, :]
            outs.append(_dot(p_scr[b % 2, h], vh) * inv[h])
        o_ref[b * WINDOW:(b + 1) * WINDOW, :] = jnp.concatenate(outs, axis=-1).astype(BF16)

    scores(0)
    for b in range(ATT_NB):
        if b + 1 < ATT_NB:
            scores(b + 1)
        weighted(b, softmax(b))


def _attention(sink, q, k, v, bias, bsz, seq, layer):
    n_blocks = seq // WINDOW
    steps = n_blocks // ATT_NB
    rows = ATT_NB * WINDOW
    return pl.pallas_call(
        functools.partial(_attn_kernel, n_blocks=n_blocks),
        name="band_attn",
        grid=(bsz, steps),
        in_specs=[
            pl.BlockSpec(memory_space=pltpu.SMEM),
            pl.BlockSpec((rows, ATT_W), lambda b, n: (b * steps + n, 0)),
            pl.BlockSpec((1, seq, KV_W), lambda b, n: (b, 0, 0)),
            pl.BlockSpec((1, seq, KV_W), lambda b, n: (b, 0, 0)),
            pl.BlockSpec((3, N_HEADS, WINDOW, 3 * WINDOW), lambda b, n: (0, 0, 0, 0)),
        ],
        out_specs=pl.BlockSpec((rows, ATT_W), lambda b, n: (b * steps + n, 0)),
        out_shape=jax.ShapeDtypeStruct((bsz * seq, ATT_W), BF16),
        scratch_shapes=[pltpu.VMEM((2, N_HEADS, WINDOW, 3 * WINDOW), F32),
                        pltpu.VMEM((2, N_HEADS, WINDOW, 3 * WINDOW), BF16)],
        compiler_params=pltpu.CompilerParams(dimension_semantics=("arbitrary", "arbitrary"),
                                             vmem_limit_bytes=VMEM_LIMIT),
    )(sink[layer], q, k.reshape(bsz, seq, KV_W), v.reshape(bsz, seq, KV_W), bias)


def _attn_bias(seq):
    q_idx = jnp.arange(WINDOW)[:, None]
    c_idx = jnp.arange(3 * WINDOW)[None, :]
    dist = jnp.abs(q_idx - c_idx + WINDOW)
    slopes = jnp.exp2(-8.0 * jnp.arange(1, N_HEADS + 1, dtype=F32) / N_HEADS)
    alibi = -slopes[:, None, None] * dist.astype(F32)[None]
    in_band = (dist <= WINDOW)[None]
    blk = (c_idx // WINDOW)[None]
    variants = []
    for bad in (0, -1, 2):
        ok = in_band & (blk != bad)
        variants.append(jnp.where(ok, alibi * LOG2E, -1e30))
    return jnp.stack(variants).astype(F32)


def _swap_lane_blocks(vs):
    lane = lax.broadcasted_iota(I32, vs[0].shape, 1)
    lo_half = lane < 2 * SUB
    even_blk = (lane & SUB) == 0
    a0, a1, a2, a3 = vs
    c0 = jnp.where(lo_half, a0, pltpu.roll(a2, 2 * SUB, 1))
    c2 = jnp.where(lo_half, pltpu.roll(a0, 2 * SUB, 1), a2)
    c1 = jnp.where(lo_half, a1, pltpu.roll(a3, 2 * SUB, 1))
    c3 = jnp.where(lo_half, pltpu.roll(a1, 2 * SUB, 1), a3)
    b0 = jnp.where(even_blk, c0, pltpu.roll(c1, SUB, 1))
    b1 = jnp.where(even_blk, pltpu.roll(c0, LANES - SUB, 1), c1)
    b2 = jnp.where(even_blk, c2, pltpu.roll(c3, SUB, 1))
    b3 = jnp.where(even_blk, pltpu.roll(c2, LANES - SUB, 1), c3)
    return [b0, b1, b2, b3]


def _ssm_kernel(u_ref, t_ref, ms_ref, mo_ref, pw_ref, dsk_ref, y_ref, *, chunks_per_seq):
    n_grp = u_ref.shape[0] // SSM_CH
    nbc = u_ref.shape[1]
    is_fwd = lax.broadcasted_iota(I32, (nbc, LANES), 1) < SSM_P
    c_idx = lax.broadcasted_iota(I32, (nbc, LANES), 0) & (chunks_per_seq - 1)

    def cmul(x, a):
        return a[0] * x[0] - a[1] * x[1], a[0] * x[1] + a[1] * x[0]

    def cadd(x, y):
        return x[0] + y[0], x[1] + y[1]

    def shifted(x, dd):
        ok_f = jnp.logical_and(is_fwd, c_idx >= dd)
        ok_b = jnp.logical_and(jnp.logical_not(is_fwd), c_idx < chunks_per_seq - dd)
        return tuple(jnp.where(ok_f, pltpu.roll(t, dd, 0), jnp.where(ok_b, pltpu.roll(t, nbc - dd, 0), 0.0))
                     for t in x)

    def gather(g):
        slabs = [u_ref[g * SSM_CH + h] for h in range(SSM_CH)]
        cols = [_swap_lane_blocks(slabs[4 * q:4 * q + 4]) for q in range(SSM_CH // 4)]
        pieces = [jnp.concatenate([cols[q][i] for q in range(SSM_CH // 4)], axis=1)
                  for i in range(N_SUB)]
        return jnp.concatenate(pieces, axis=0)

    def entering_states(g, s_all):
        def prow(r):
            return pw_ref[g, 0, r:r + 1, :], pw_ref[g, 1, r:r + 1, :]

        def mirror(k):
            lo, hi = k * nbc, (N_SUB - 1 - k) * nbc
            return tuple(jnp.where(is_fwd, s_all[lo:lo + nbc, c:c + LANES], s_all[hi:hi + nbc, c:c + LANES])
                         for c in (0, LANES))

        a32 = prow(0)
        sk = [mirror(k) for k in range(N_SUB)]
        zero = jnp.zeros((nbc, LANES), F32)
        pk = [(zero, zero)]
        for k in range(N_SUB - 1):
            pk.append(cadd(cmul(pk[k], a32), sk[k]))
        x = cadd(cmul(pk[N_SUB - 1], a32), sk[N_SUB - 1])
        for kk in range(_N_SCAN_STEPS):
            x = cadd(x, cmul(shifted(x, 1 << kk), prow(3 + kk)))
        ent = shifted(x, 1)
        qk = [cadd(pk[k], cmul(ent, prow(k - 1)) if k > 0 else ent) for k in range(N_SUB)]
        xin = []
        for i in range(N_SUB):
            j = N_SUB - 1 - i
            xin.append(jnp.concatenate([jnp.where(is_fwd, qk[i][c], qk[j][c]) for c in (0, 1)], axis=1))
        return jnp.concatenate(xin, axis=0).astype(BF16)

    def scatter(g, act):
        for q in range(SSM_CH // 4):
            back = _swap_lane_blocks([act[i * nbc:(i + 1) * nbc, q * LANES:(q + 1) * LANES]
                                      for i in range(N_SUB)])
            for r in range(4):
                y_ref[g * SSM_CH + 4 * q + r] = back[r]

    grp = range(n_grp)
    ust = [gather(g) for g in grp]
    ub = [u.astype(BF16) for u in ust]
    z = [_dot(ub[g], t_ref[g]) for g in grp]
    s_all = [_dot(ub[g], ms_ref[g]) for g in grp]
    xin = [entering_states(g, s_all[g]) for g in grp]
    for g in grp:
        y = z[g] + _dot(xin[g], mo_ref[g]) + dsk_ref[g] * ust[g]
        scatter(g, jax.nn.gelu(y))


def _ssm(u3, t_all, ms_all, mo_all, pw_all, dsk_all, layer, chunks_per_seq):
    nbc = u3.shape[1]
    base = layer * (SSM_G // SSM_GB)
    sel3 = lambda g: (base + g, 0, 0)
    slab_spec = pl.BlockSpec((SSM_GB * SSM_CH, nbc, LANES), lambda g: (g, 0, 0))
    return pl.pallas_call(
        functools.partial(_ssm_kernel, chunks_per_seq=chunks_per_seq),
        name="ssm_mix",
        grid=(SSM_G // SSM_GB,),
        in_specs=[
            slab_spec,
            pl.BlockSpec((SSM_GB, CW, CW), sel3),
            pl.BlockSpec((SSM_GB, CW, 2 * LANES), sel3),
            pl.BlockSpec((SSM_GB, 2 * LANES, CW), sel3),
            pl.BlockSpec((SSM_GB, 2, _POW_ROWS, LANES), lambda g: (base + g, 0, 0, 0)),
            pl.BlockSpec((SSM_GB, 1, CW), sel3),
        ],
        out_specs=slab_spec,
        out_shape=jax.ShapeDtypeStruct((SSM_W, nbc, LANES), F32),
        compiler_params=pltpu.CompilerParams(dimension_semantics=("arbitrary",),
                                             vmem_limit_bytes=VMEM_LIMIT),
    )(u3, t_all, ms_all, mo_all, pw_all, dsk_all)


def _outffn_kernel(x_ref, att_ref, y_ref, wglu_ref, wout_ref, g2_ref, w1_ref, w2_ref, o_ref):
    first = (pl.program_id(0) % (Y_CHUNKS * LANES // TM_FFN)) * (TM_FFN // LANES)
    y2d = y_ref.reshape(SSM_W * Y_CHUNKS, LANES)
    ys = [y2d[pl.ds(first + j, SSM_W, stride=Y_CHUNKS), :].T
          for j in range(TM_FFN // LANES)]
    y = jnp.concatenate(ys, axis=0).astype(BF16)
    gl = _dot(y, wglu_ref[...])
    ssm = gl[:, :SSM_W] * jax.nn.sigmoid(gl[:, SSM_W:])
    x = x_ref[...] + _dot(att_ref[...], wout_ref[:ATT_W, :]) \
        + _dot(ssm.astype(BF16), wout_ref[ATT_W:, :])
    ms = jnp.mean(x * x, axis=-1, keepdims=True)
    hn = (x * lax.rsqrt(ms + EPS) * g2_ref[...]).astype(BF16)
    hid = jnp.maximum(_dot(hn, w1_ref[...]), 0.0)
    o_ref[...] = x + _dot((hid * hid).astype(BF16), w2_ref[...])


def _outffn(x, att, y3, wglu, wout, g2, w1, w2, layer):
    t = x.shape[0]
    sel = lambda i: (layer, 0, 0)
    single = pl.Buffered(1)
    return pl.pallas_call(
        _outffn_kernel,
        name="out_ffn",
        grid=(t // TM_FFN,),
        in_specs=[
            pl.BlockSpec((TM_FFN, D_MODEL), lambda i: (i, 0)),
            pl.BlockSpec((TM_FFN, ATT_W), lambda i: (i, 0)),
            pl.BlockSpec((SSM_W, Y_CHUNKS, LANES), lambda i: (0, i * TM_FFN // (Y_CHUNKS * LANES), 0)),
            pl.BlockSpec((None, SSM_W, 2 * SSM_W), sel, pipeline_mode=single),
            pl.BlockSpec((None, D_MODEL, D_MODEL), sel, pipeline_mode=single),
            pl.BlockSpec((None, 1, D_MODEL), sel),
            pl.BlockSpec((None, D_MODEL, D_FF), sel, pipeline_mode=single),
            pl.BlockSpec((None, D_FF, D_MODEL), sel, pipeline_mode=single),
        ],
        out_specs=pl.BlockSpec((TM_FFN, D_MODEL), lambda i: (i, 0)),
        out_shape=jax.ShapeDtypeStruct((t, D_MODEL), F32),
        compiler_params=pltpu.CompilerParams(dimension_semantics=("arbitrary",),
                                             vmem_limit_bytes=VMEM_LIMIT),
    )(x, att, y3, wglu, wout, g2, w1, w2)


def _block_diag_mean(width):
    idx = jnp.arange(width) // HEAD_DIM
    return jnp.where(idx[:, None] == idx[None, :], 1.0 / HEAD_DIM, 0.0).astype(BF16)


def _prep_operands(lam_re, lam_im, log_dt, b_re, b_im, c_re, c_im, d_skip):
    depth = lam_re.shape[0]
    lg = depth * SSM_G
    both = lambda a: jnp.concatenate([a[:, 0], a[:, 1]], axis=-1)
    ldt = jnp.broadcast_to(log_dt[..., None], lam_re.shape)
    lam = jnp.stack([both(lam_re), both(lam_im), both(ldt)], axis=2).reshape(lg, 3, LANES)
    bt = jnp.stack([b_re, b_im], axis=2).transpose(0, 1, 2, 4, 3)
    bt = jnp.concatenate([bt, bt], axis=-1).reshape(lg, 2, SSM_CH, LANES)
    cc = jnp.stack([both(c_re), both(c_im)], axis=2).reshape(lg, 2, SSM_CH, LANES)
    nrow = jnp.zeros((_POW_ROWS,), F32).at[:len(_POW_N)].set(jnp.array(_POW_N, F32))
    nrow = jnp.broadcast_to(nrow[:, None], (_POW_ROWS, LANES))
    dsk = jnp.repeat(d_skip.reshape(lg, 1, SSM_CH), SUB, axis=-1)
    return list(_ssm_prep(lam, bt, cc, nrow)) + [dsk]


def kernel(x, norm1, w_in, q_gain, k_gain, sink, lam_re, lam_im, log_dt, b_re, b_im, c_re, c_im,
           d_skip, w_glu, w_out, norm2, w_ff1, w_ff2):
    bsz, seq, _ = x.shape
    depth = norm1.shape[0]
    tokens = bsz * seq
    chunks_per_seq = seq // LANES
    assert seq % (ATT_NB * WINDOW) == 0 and tokens % TM_IN == 0 and tokens % TM_FFN == 0
    assert TM_IN == Y_CHUNKS * LANES and (Y_CHUNKS * LANES) % TM_FFN == 0
    assert chunks_per_seq == 1 << _N_SCAN_STEPS and max(_POW_N) == LANES * chunks_per_seq // 2

    t_all, ms_all, mo_all, pw_all, dsk_all = _prep_operands(lam_re, lam_im, log_dt, b_re, b_im,
                                                           c_re, c_im, d_skip)
    wqkv = w_in[:, :, :QKV_W].astype(BF16)
    wut = w_in[:, :, QKV_W:].transpose(0, 2, 1).astype(BF16)
    wglu = w_glu.astype(BF16)
    wout = w_out.astype(BF16)
    w1 = w_ff1.astype(BF16)
    w2 = w_ff2.astype(BF16)
    g1 = norm1[:, None, :]
    g2 = norm2[:, None, :]
    qg = (jnp.tile(q_gain, (1, N_HEADS)) * (LOG2E / math.sqrt(HEAD_DIM)))[:, None, :]
    kg = jnp.tile(k_gain, (1, N_KV))[:, None, :]
    bias = _attn_bias(seq)
    bdq = _block_diag_mean(ATT_W)
    bdk = _block_diag_mean(KV_W)

    xt = x.reshape(tokens, D_MODEL)
    for layer in range(depth):
        q, k, v, u3 = _inproj(xt, g1, wqkv, wut, qg, kg, bdq, bdk, layer)
        att = _attention(sink, q, k, v, bias, bsz, seq, layer)
        y3 = _ssm(u3, t_all, ms_all, mo_all, pw_all, dsk_all, layer, chunks_per_seq)
        xt = _outffn(xt, att, y3, wglu, wout, g2, w1, w2, layer)
    return xt.reshape(bsz, seq, D_MODEL)
```

```python
import functools
import math

import jax
import jax.numpy as jnp
from jax import lax
from jax.experimental import pallas as pl
from jax.experimental.pallas import tpu as pltpu

D_MODEL = 1024
N_HEADS = 8
N_KV = 2
GQA = N_HEADS // N_KV
HEAD_DIM = 64
ATT_W = N_HEADS * HEAD_DIM
KV_W = N_KV * HEAD_DIM
QKV_W = ATT_W + 2 * KV_W
WINDOW = 128
SSM_W = 512
SSM_CH = 16
SSM_G = SSM_W // SSM_CH
SSM_P = 64
D_FF = 4 * D_MODEL
EPS = 1e-6
LOG2E = math.log2(math.e)

LANES = 128
SUB = 32
N_SUB = LANES // SUB
CW = SSM_CH * SUB
VMEM_LIMIT = 56 * 1024 * 1024

TM_IN = 1024
TM_HALF = 512
TM_FFN = 512
Y_CHUNKS = 8
ATT_NB = 4
SSM_GB = 2
PREP_GB = 4

_POW_N = (32, 64, 96, 128, 256, 512, 1024, 2048, 4096)
_POW_ROWS = 16
_N_SCAN_STEPS = 6

F32 = jnp.float32
BF16 = jnp.bfloat16
I32 = jnp.int32


def _dot(a, b):
    return jnp.dot(a, b, preferred_element_type=F32)


def _dot_nt(a, b, precision=None):
    return lax.dot_general(a, b, (((1,), (1,)), ((), ())), precision=precision,
                           preferred_element_type=F32)


def _prep_kernel(lam_ref, bt_ref, c_ref, t_ref, ms_ref, mo_ref, pw_ref,
                 g1r_ref, g1i_ref, mtr_ref, mti_ref):
    grp = range(lam_ref.shape[0])
    s_idx = lax.broadcasted_iota(I32, (SUB, LANES), 0)
    is_fwd = lax.broadcasted_iota(I32, (SUB, LANES), 1) < SSM_P
    fwd16 = lax.broadcasted_iota(I32, (SSM_CH, LANES), 1) < SSM_P
    e_t = jnp.where(is_fwd, s_idx, SUB - 1 - s_idx)

    def cmul(x, y):
        return x[0] * y[0] - x[1] * y[1], x[0] * y[1] + x[1] * y[0]

    def discretise(g):
        lr = lam_ref[g, 0:1, :]
        li = lam_ref[g, 1:2, :]
        dt = jnp.exp(lam_ref[g, 2:3, :])
        mag = jnp.exp(lr * dt)
        abr = mag * jnp.cos(li * dt)
        abi = mag * jnp.sin(li * dt)
        den = lr * lr + li * li
        zr = ((abr - 1.0) * lr + abi * li) / den
        zi = (abi * lr - (abr - 1.0) * li) / den
        btr = bt_ref[g, 0]
        bti = bt_ref[g, 1]
        sq = [(abr, abi)]
        for _ in range(max(_POW_N).bit_length() - 1):
            sq.append(cmul(sq[-1], sq[-1]))
        return sq, zr * btr - zi * bti, zr * bti + zi * btr

    def int_power(sq, e):
        acc = None
        for bit in range(SUB.bit_length() - 1):
            on = (e & (1 << bit)) != 0
            fac = (jnp.where(on, sq[bit][0], 1.0), jnp.where(on, sq[bit][1], 0.0))
            acc = fac if acc is None else cmul(acc, fac)
        return acc

    disc = [discretise(g) for g in grp]
    tpow = [int_power(disc[g][0], e_t) for g in grp]
    mspow = [int_power(disc[g][0], SUB - 1 - e_t) for g in grp]
    mopow = [cmul(tpow[g], disc[g][0][0]) for g in grp]

    for g in grp:
        _, bbr, bbi = disc[g]
        cr = c_ref[g, 0]
        ci = c_ref[g, 1]
        (tr, ti), (msr, msi), (mor, moi) = tpow[g], mspow[g], mopow[g]
        for h in range(SSM_CH):
            rows = slice(h * SUB, (h + 1) * SUB)
            b_r = bbr[h:h + 1, :]
            b_i = bbi[h:h + 1, :]
            ms_ref[g, rows, :LANES] = (msr * b_r - msi * b_i).astype(BF16)
            ms_ref[g, rows, LANES:] = (msr * b_i + msi * b_r).astype(BF16)
            c_r = cr[h:h + 1, :]
            c_i = ci[h:h + 1, :]
            g1r_ref[g, rows, :] = tr * c_r - ti * c_i
            g1i_ref[g, rows, :] = tr * c_i + ti * c_r
            mtr_ref[g, rows, :] = mor * c_r - moi * c_i
            mti_ref[g, rows, :] = -(mor * c_i + moi * c_r)
    for g in grp:
        mo_ref[g, :LANES, :] = mtr_ref[g].T.astype(BF16)
        mo_ref[g, LANES:, :] = mti_ref[g].T.astype(BF16)

    hp = lax.Precision.HIGHEST
    kts = []
    for g in grp:
        _, bbr, bbi = disc[g]
        g1r = g1r_ref[g]
        g1i = g1i_ref[g]
        kts.append([_dot_nt(jnp.where(sel, bbr, 0.0), g1r, precision=hp)
                    - _dot_nt(jnp.where(sel, bbi, 0.0), g1i, precision=hp)
                    for sel in (fwd16, jnp.logical_not(fwd16))])

    for g in grp:
        sq = disc[g][0]
        pw_ref[g] = jnp.zeros((2, _POW_ROWS, LANES), F32)
        for r, n in enumerate(_POW_N):
            acc = None
            for bit in range(n.bit_length()):
                if n >> bit & 1:
                    acc = sq[bit] if acc is None else cmul(acc, sq[bit])
            pw_ref[g, 0, r:r + 1, :] = acc[0]
            pw_ref[g, 1, r:r + 1, :] = acc[1]

    s_lane = lax.broadcasted_iota(I32, (SUB, CW), 1) & (SUB - 1)
    s_row = lax.broadcasted_iota(I32, (SUB, CW), 0)
    fwd_mask = s_lane >= s_row
    tau0 = (lax.broadcasted_iota(I32, (SSM_CH, CW), 1) & (SUB - 1)) == 0
    for g in grp:
        kfd = kts[g][0] + jnp.where(tau0, pltpu.roll(kts[g][1], CW - (SUB - 1), 1), 0.0)
        for h in range(SSM_CH):
            kf = jnp.broadcast_to(kfd[h:h + 1, :], (SUB, CW))
            kb = jnp.broadcast_to(kts[g][1][h:h + 1, :], (SUB, CW))
            rf = pltpu.roll(kf, 0, 1, stride=1, stride_axis=0)
            rb = pltpu.roll(kb, CW - (SUB - 1), 1, stride=1, stride_axis=0)
            t_ref[g, h * SUB:(h + 1) * SUB, :] = jnp.where(fwd_mask, rf, rb).astype(BF16)


def _ssm_prep(lam, bt, c):
    lg = lam.shape[0]
    sel3 = lambda i: (i, 0, 0)
    sel4 = lambda i: (i, 0, 0, 0)
    return pl.pallas_call(
        _prep_kernel,
        name="ssm_prep",
        grid=(lg // PREP_GB,),
        in_specs=[
            pl.BlockSpec((PREP_GB, 3, LANES), sel3),
            pl.BlockSpec((PREP_GB, 2, SSM_CH, LANES), sel4),
            pl.BlockSpec((PREP_GB, 2, SSM_CH, LANES), sel4),
        ],
        out_specs=[
            pl.BlockSpec((PREP_GB, CW, CW), sel3),
            pl.BlockSpec((PREP_GB, CW, 2 * LANES), sel3),
            pl.BlockSpec((PREP_GB, 2 * LANES, CW), sel3),
            pl.BlockSpec((PREP_GB, 2, _POW_ROWS, LANES), sel4),
        ],
        out_shape=[
            jax.ShapeDtypeStruct((lg, CW, CW), BF16),
            jax.ShapeDtypeStruct((lg, CW, 2 * LANES), BF16),
            jax.ShapeDtypeStruct((lg, 2 * LANES, CW), BF16),
            jax.ShapeDtypeStruct((lg, 2, _POW_ROWS, LANES), F32),
        ],
        scratch_shapes=[pltpu.VMEM((PREP_GB, CW, LANES), F32)] * 4,
        compiler_params=pltpu.CompilerParams(dimension_semantics=("arbitrary",)),
    )(lam, bt, c)


def _inproj_kernel(x_ref, g1_ref, wqkv_ref, wut_ref, qg_ref, kg_ref, bdq_ref, bdk_ref,
                   q_ref, k_ref, v_ref, u_ref):
    def head_norm(t, bd_ref, gain):
        t2 = t * t
        hi = t2.astype(BF16)
        lo = (t2 - hi.astype(F32)).astype(BF16)
        m = _dot(hi, bd_ref[...]) + _dot(lo, bd_ref[...])
        return t * lax.rsqrt(m + EPS) * gain

    halves = [slice(r * TM_HALF, (r + 1) * TM_HALF) for r in range(TM_IN // TM_HALF)]
    hn = []
    for rows in halves:
        x = x_ref[rows, :]
        ms = jnp.mean(x * x, axis=-1, keepdims=True)
        hn.append((x * lax.rsqrt(ms + EPS) * g1_ref[...]).astype(BF16))
    z = [_dot(h, wqkv_ref[...]) for h in hn]
    ut = [_dot_nt(wut_ref[...], h) for h in hn]
    for rows, zz in zip(halves, z):
        q_ref[rows, :] = head_norm(zz[:, :ATT_W], bdq_ref, qg_ref[...]).astype(BF16)
        k_ref[rows, :] = head_norm(zz[:, ATT_W:ATT_W + KV_W], bdk_ref, kg_ref[...]).astype(BF16)
        v_ref[rows, :] = zz[:, ATT_W + KV_W:].astype(BF16)
    chunks = TM_IN // LANES
    u2d = u_ref.reshape(SSM_W * chunks, LANES)
    for r, utr in enumerate(ut):
        for jj in range(TM_HALF // LANES):
            j = r * (TM_HALF // LANES) + jj
            u2d[pl.ds(j, SSM_W, stride=chunks), :] = utr[:, jj * LANES:(jj + 1) * LANES]


def _inproj(x, g1, wqkv, wut, qg, kg, bdq, bdk, layer):
    t = x.shape[0]
    const = lambda i: (0, 0)
    sel = lambda i: (layer, 0, 0)
    return pl.pallas_call(
        _inproj_kernel,
        name="inproj",
        grid=(t // TM_IN,),
        in_specs=[
            pl.BlockSpec((TM_IN, D_MODEL), lambda i: (i, 0)),
            pl.BlockSpec((None, 1, D_MODEL), sel),
            pl.BlockSpec((None, D_MODEL, QKV_W), sel),
            pl.BlockSpec((None, SSM_W, D_MODEL), sel),
            pl.BlockSpec((None, 1, ATT_W), sel),
            pl.BlockSpec((None, 1, KV_W), sel),
            pl.BlockSpec((ATT_W, ATT_W), const),
            pl.BlockSpec((KV_W, KV_W), const),
        ],
        out_specs=[
            pl.BlockSpec((TM_IN, ATT_W), lambda i: (i, 0)),
            pl.BlockSpec((TM_IN, KV_W), lambda i: (i, 0)),
            pl.BlockSpec((TM_IN, KV_W), lambda i: (i, 0)),
            pl.BlockSpec((SSM_W, TM_IN // LANES, LANES), lambda i: (0, i, 0)),
        ],
        out_shape=[
            jax.ShapeDtypeStruct((t, ATT_W), BF16),
            jax.ShapeDtypeStruct((t, KV_W), BF16),
            jax.ShapeDtypeStruct((t, KV_W), BF16),
            jax.ShapeDtypeStruct((SSM_W, t // LANES, LANES), F32),
        ],
        compiler_params=pltpu.CompilerParams(dimension_semantics=("arbitrary",),
                                             vmem_limit_bytes=VMEM_LIMIT),
    )(x, g1, wqkv, wut, qg, kg, bdq, bdk)


def _attn_kernel(sink_ref, q_ref, k_ref, v_ref, bias_ref, o_ref, s_scr, p_scr, *, n_blocks):
    n0 = pl.program_id(1) * ATT_NB
    prev = pl.multiple_of(jnp.maximum(n0 - 1, 0) * WINDOW, WINDOW)
    mid = pl.multiple_of(n0 * WINDOW, WINDOW)
    nxt = pl.multiple_of(jnp.minimum(n0 + ATT_NB, n_blocks - 1) * WINDOW, WINDOW)

    def window(ref):
        return jnp.concatenate([ref[0, pl.ds(prev, WINDOW), :], ref[0, pl.ds(mid, ATT_NB * WINDOW), :],
                                ref[0, pl.ds(nxt, WINDOW), :]], axis=0)

    kw = window(k_ref)
    vw = window(v_ref)
    kj = [kw[:, j * HEAD_DIM:(j + 1) * HEAD_DIM] for j in range(N_KV)]
    vj = [vw[:, j * HEAD_DIM:(j + 1) * HEAD_DIM] for j in range(N_KV)]

    def scores(b):
        n = n0 + b
        var = jnp.where(n == 0, 0, jnp.where(n == n_blocks - 1, 2, 1))
        qb = q_ref[b * WINDOW:(b + 1) * WINDOW, :]
        for h in range(N_HEADS):
            kh = kj[h // GQA][b * WINDOW:(b + 3) * WINDOW, :]
            s_scr[b % 2, h] = _dot_nt(qb[:, h * HEAD_DIM:(h + 1) * HEAD_DIM], kh) + bias_ref[var, h]

    def softmax(b):
        inv = []
        for h in range(N_HEADS):
            s = s_scr[b % 2, h]
            sk = sink_ref[h] * LOG2E
            m = jnp.maximum(jnp.max(s, axis=-1, keepdims=True), sk)
            p = jnp.exp2(s - m)
            inv.append(1.0 / (jnp.sum(p, axis=-1, keepdims=True) + jnp.exp2(sk - m)))
            p_scr[b % 2, h] = p.astype(BF16)
        return inv

    def weighted(b, inv):
        outs = []
        for h in range(N_HEADS):
            vh = vj[h // GQA][b * WINDOW:(b + 3) * WINDOW, :]
            outs.append(_dot(p_scr[b % 2, h], vh) * inv[h])
        o_ref[b * WINDOW:(b + 1) * WINDOW, :] = jnp.concatenate(outs, axis=-1).astype(BF16)

    scores(0)
    for b in range(ATT_NB):
        if b + 1 < ATT_NB:
            scores(b + 1)
        weighted(b, softmax(b))


def _attention(sink, q, k, v, bias, bsz, seq, layer):
    n_blocks = seq // WINDOW
    steps = n_blocks // ATT_NB
    rows = ATT_NB * WINDOW
    return pl.pallas_call(
        functools.partial(_attn_kernel, n_blocks=n_blocks),
        name="band_attn",
        grid=(bsz, steps),
        in_specs=[
            pl.BlockSpec(memory_space=pltpu.SMEM),
            pl.BlockSpec((rows, ATT_W), lambda b, n: (b * steps + n, 0)),
            pl.BlockSpec((1, seq, KV_W), lambda b, n: (b, 0, 0)),
            pl.BlockSpec((1, seq, KV_W), lambda b, n: (b, 0, 0)),
            pl.BlockSpec((3, N_HEADS, WINDOW, 3 * WINDOW), lambda b, n: (0, 0, 0, 0)),
        ],
        out_specs=pl.BlockSpec((rows, ATT_W), lambda b, n: (b * steps + n, 0)),
        out_shape=jax.ShapeDtypeStruct((bsz * seq, ATT_W), BF16),
        scratch_shapes=[pltpu.VMEM((2, N_HEADS, WINDOW, 3 * WINDOW), F32),
                        pltpu.VMEM((2, N_HEADS, WINDOW, 3 * WINDOW), BF16)],
        compiler_params=pltpu.CompilerParams(dimension_semantics=("arbitrary", "arbitrary"),
                                             vmem_limit_bytes=VMEM_LIMIT),
    )(sink[layer], q, k.reshape(bsz, seq, KV_W), v.reshape(bsz, seq, KV_W), bias)


def _attn_bias(seq):
    q_idx = jnp.arange(WINDOW)[:, None]
    c_idx = jnp.arange(3 * WINDOW)[None, :]
    dist = jnp.abs(q_idx - c_idx + WINDOW)
    slopes = jnp.exp2(-8.0 * jnp.arange(1, N_HEADS + 1, dtype=F32) / N_HEADS)
    alibi = -slopes[:, None, None] * dist.astype(F32)[None]
    in_band = (dist <= WINDOW)[None]
    blk = (c_idx // WINDOW)[None]
    variants = []
    for bad in (0, -1, 2):
        ok = in_band & (blk != bad)
        variants.append(jnp.where(ok, alibi * LOG2E, -1e30))
    return jnp.stack(variants).astype(F32)


def _swap_lane_blocks(vs):
    lane = lax.broadcasted_iota(I32, vs[0].shape, 1)
    lo_half = lane < 2 * SUB
    even_blk = (lane & SUB) == 0
    a0, a1, a2, a3 = vs
    c0 = jnp.where(lo_half, a0, pltpu.roll(a2, 2 * SUB, 1))
    c2 = jnp.where(lo_half, pltpu.roll(a0, 2 * SUB, 1), a2)
    c1 = jnp.where(lo_half, a1, pltpu.roll(a3, 2 * SUB, 1))
    c3 = jnp.where(lo_half, pltpu.roll(a1, 2 * SUB, 1), a3)
    b0 = jnp.where(even_blk, c0, pltpu.roll(c1, SUB, 1))
    b1 = jnp.where(even_blk, pltpu.roll(c0, LANES - SUB, 1), c1)
    b2 = jnp.where(even_blk, c2, pltpu.roll(c3, SUB, 1))
    b3 = jnp.where(even_blk, pltpu.roll(c2, LANES - SUB, 1), c3)
    return [b0, b1, b2, b3]


def _ssm_kernel(u_ref, t_ref, ms_ref, mo_ref, pw_ref, dsk_ref, y_ref, *, chunks_per_seq):
    n_grp = u_ref.shape[0] // SSM_CH
    nbc = u_ref.shape[1]
    is_fwd = lax.broadcasted_iota(I32, (nbc, LANES), 1) < SSM_P
    c_idx = lax.broadcasted_iota(I32, (nbc, LANES), 0) & (chunks_per_seq - 1)

    def cmul(x, a):
        return a[0] * x[0] - a[1] * x[1], a[0] * x[1] + a[1] * x[0]

    def cadd(x, y):
        return x[0] + y[0], x[1] + y[1]

    def shifted(x, dd):
        ok_f = jnp.logical_and(is_fwd, c_idx >= dd)
        ok_b = jnp.logical_and(jnp.logical_not(is_fwd), c_idx < chunks_per_seq - dd)
        return tuple(jnp.where(ok_f, pltpu.roll(t, dd, 0), jnp.where(ok_b, pltpu.roll(t, nbc - dd, 0), 0.0))
                     for t in x)

    def gather(g):
        slabs = [u_ref[g * SSM_CH + h] for h in range(SSM_CH)]
        cols = [_swap_lane_blocks(slabs[4 * q:4 * q + 4]) for q in range(SSM_CH // 4)]
        pieces = [jnp.concatenate([cols[q][i] for q in range(SSM_CH // 4)], axis=1)
                  for i in range(N_SUB)]
        return jnp.concatenate(pieces, axis=0)

    def entering_states(g, s_all):
        def prow(r):
            return pw_ref[g, 0, r:r + 1, :], pw_ref[g, 1, r:r + 1, :]

        def mirror(k):
            lo, hi = k * nbc, (N_SUB - 1 - k) * nbc
            return tuple(jnp.where(is_fwd, s_all[lo:lo + nbc, c:c + LANES], s_all[hi:hi + nbc, c:c + LANES])
                         for c in (0, LANES))

        a32 = prow(0)
        sk = [mirror(k) for k in range(N_SUB)]
        zero = jnp.zeros((nbc, LANES), F32)
        pk = [(zero, zero)]
        for k in range(N_SUB - 1):
            pk.append(cadd(cmul(pk[k], a32), sk[k]))
        x = cadd(cmul(pk[N_SUB - 1], a32), sk[N_SUB - 1])
        for kk in range(_N_SCAN_STEPS):
            x = cadd(x, cmul(shifted(x, 1 << kk), prow(3 + kk)))
        ent = shifted(x, 1)
        qk = [cadd(pk[k], cmul(ent, prow(k - 1)) if k > 0 else ent) for k in range(N_SUB)]
        xin = []
        for i in range(N_SUB):
            j = N_SUB - 1 - i
            xin.append(jnp.concatenate([jnp.where(is_fwd, qk[i][c], qk[j][c]) for c in (0, 1)], axis=1))
        return jnp.concatenate(xin, axis=0).astype(BF16)

    def scatter(g, act):
        for q in range(SSM_CH // 4):
            back = _swap_lane_blocks([act[i * nbc:(i + 1) * nbc, q * LANES:(q + 1) * LANES]
                                      for i in range(N_SUB)])
            for r in range(4):
                y_ref[g * SSM_CH + 4 * q + r] = back[r]

    grp = range(n_grp)
    ust = [gather(g) for g in grp]
    ub = [u.astype(BF16) for u in ust]
    z = [_dot(ub[g], t_ref[g]) for g in grp]
    s_all = [_dot(ub[g], ms_ref[g]) for g in grp]
    xin = [entering_states(g, s_all[g]) for g in grp]
    for g in grp:
        y = z[g] + _dot(xin[g], mo_ref[g]) + dsk_ref[g] * ust[g]
        scatter(g, jax.nn.gelu(y))


def _ssm(u3, t_all, ms_all, mo_all, pw_all, dsk_all, layer, chunks_per_seq):
    nbc = u3.shape[1]
    base = layer * (SSM_G // SSM_GB)
    sel3 = lambda g: (base + g, 0, 0)
    slab_spec = pl.BlockSpec((SSM_GB * SSM_CH, nbc, LANES), lambda g: (g, 0, 0))
    return pl.pallas_call(
        functools.partial(_ssm_kernel, chunks_per_seq=chunks_per_seq),
        name="ssm_mix",
        grid=(SSM_G // SSM_GB,),
        in_specs=[
            slab_spec,
            pl.BlockSpec((SSM_GB, CW, CW), sel3),
            pl.BlockSpec((SSM_GB, CW, 2 * LANES), sel3),
            pl.BlockSpec((SSM_GB, 2 * LANES, CW), sel3),
            pl.BlockSpec((SSM_GB, 2, _POW_ROWS, LANES), lambda g: (base + g, 0, 0, 0)),
            pl.BlockSpec((SSM_GB, 1, CW), sel3),
        ],
        out_specs=slab_spec,
        out_shape=jax.ShapeDtypeStruct((SSM_W, nbc, LANES), F32),
        compiler_params=pltpu.CompilerParams(dimension_semantics=("arbitrary",),
                                             vmem_limit_bytes=VMEM_LIMIT),
    )(u3, t_all, ms_all, mo_all, pw_all, dsk_all)


def _outffn_kernel(x_ref, att_ref, y_ref, wglu_ref, wout_ref, g2_ref, w1_ref, w2_ref, o_ref):
    first = (pl.program_id(0) % (Y_CHUNKS * LANES // TM_FFN)) * (TM_FFN // LANES)
    y2d = y_ref.reshape(SSM_W * Y_CHUNKS, LANES)
    ys = [y2d[pl.ds(first + j, SSM_W, stride=Y_CHUNKS), :].T
          for j in range(TM_FFN // LANES)]
    y = jnp.concatenate(ys, axis=0).astype(BF16)
    gl = _dot(y, wglu_ref[...])
    ssm = gl[:, :SSM_W] * jax.nn.sigmoid(gl[:, SSM_W:])
    x = x_ref[...] + _dot(att_ref[...], wout_ref[:ATT_W, :]) \
        + _dot(ssm.astype(BF16), wout_ref[ATT_W:, :])
    ms = jnp.mean(x * x, axis=-1, keepdims=True)
    hn = (x * lax.rsqrt(ms + EPS) * g2_ref[...]).astype(BF16)
    hid = jnp.maximum(_dot(hn, w1_ref[...]), 0.0)
    o_ref[...] = x + _dot((hid * hid).astype(BF16), w2_ref[...])


def _outffn(x, att, y3, wglu, wout, g2, w1, w2, layer):
    t = x.shape[0]
    sel = lambda i: (layer, 0, 0)
    single = pl.Buffered(1)
    return pl.pallas_call(
        _outffn_kernel,
        name="out_ffn",
        grid=(t // TM_FFN,),
        in_specs=[
            pl.BlockSpec((TM_FFN, D_MODEL), lambda i: (i, 0)),
            pl.BlockSpec((TM_FFN, ATT_W), lambda i: (i, 0)),
            pl.BlockSpec((SSM_W, Y_CHUNKS, LANES), lambda i: (0, i * TM_FFN // (Y_CHUNKS * LANES), 0)),
            pl.BlockSpec((None, SSM_W, 2 * SSM_W), sel, pipeline_mode=single),
            pl.BlockSpec((None, D_MODEL, D_MODEL), sel, pipeline_mode=single),
            pl.BlockSpec((None, 1, D_MODEL), sel),
            pl.BlockSpec((None, D_MODEL, D_FF), sel, pipeline_mode=single),
            pl.BlockSpec((None, D_FF, D_MODEL), sel, pipeline_mode=single),
        ],
        out_specs=pl.BlockSpec((TM_FFN, D_MODEL), lambda i: (i, 0)),
        out_shape=jax.ShapeDtypeStruct((t, D_MODEL), F32),
        compiler_params=pltpu.CompilerParams(dimension_semantics=("arbitrary",),
                                             vmem_limit_bytes=VMEM_LIMIT),
    )(x, att, y3, wglu, wout, g2, w1, w2)


def _block_diag_mean(width):
    idx = jnp.arange(width) // HEAD_DIM
    return jnp.where(idx[:, None] == idx[None, :], 1.0 / HEAD_DIM, 0.0).astype(BF16)


def _prep_operands(lam_re, lam_im, log_dt, b_re, b_im, c_re, c_im, d_skip):
    depth = lam_re.shape[0]
    lg = depth * SSM_G
    both = lambda a: jnp.concatenate([a[:, 0], a[:, 1]], axis=-1)
    ldt = jnp.broadcast_to(log_dt[..., None], lam_re.shape)
    lam = jnp.stack([both(lam_re), both(lam_im), both(ldt)], axis=2).reshape(lg, 3, LANES)
    bt = jnp.stack([b_re, b_im], axis=2).transpose(0, 1, 2, 4, 3)
    bt = jnp.concatenate([bt, bt], axis=-1).reshape(lg, 2, SSM_CH, LANES)
    cc = jnp.stack([both(c_re), both(c_im)], axis=2).reshape(lg, 2, SSM_CH, LANES)
    dsk = jnp.repeat(d_skip.reshape(lg, 1, SSM_CH), SUB, axis=-1)
    return list(_ssm_prep(lam, bt, cc)) + [dsk]


def kernel(x, norm1, w_in, q_gain, k_gain, sink, lam_re, lam_im, log_dt, b_re, b_im, c_re, c_im,
           d_skip, w_glu, w_out, norm2, w_ff1, w_ff2):
    bsz, seq, _ = x.shape
    depth = norm1.shape[0]
    tokens = bsz * seq
    chunks_per_seq = seq // LANES
    assert seq % (ATT_NB * WINDOW) == 0 and tokens % TM_IN == 0 and tokens % TM_FFN == 0
    assert TM_IN == Y_CHUNKS * LANES and (Y_CHUNKS * LANES) % TM_FFN == 0
    assert chunks_per_seq == 1 << _N_SCAN_STEPS and max(_POW_N) == LANES * chunks_per_seq // 2

    t_all, ms_all, mo_all, pw_all, dsk_all = _prep_operands(lam_re, lam_im, log_dt, b_re, b_im,
                                                           c_re, c_im, d_skip)
    wqkv = w_in[:, :, :QKV_W].astype(BF16)
    wut = w_in[:, :, QKV_W:].transpose(0, 2, 1).astype(BF16)
    wglu = w_glu.astype(BF16)
    wout = w_out.astype(BF16)
    w1 = w_ff1.astype(BF16)
    w2 = w_ff2.astype(BF16)
    g1 = norm1[:, None, :]
    g2 = norm2[:, None, :]
    qg = (jnp.tile(q_gain, (1, N_HEADS)) * (LOG2E / math.sqrt(HEAD_DIM)))[:, None, :]
    kg = jnp.tile(k_gain, (1, N_KV))[:, None, :]
    bias = _attn_bias(seq)
    bdq = _block_diag_mean(ATT_W)
    bdk = _block_diag_mean(KV_W)

    xt = x.reshape(tokens, D_MODEL)
    for layer in range(depth):
        q, k, v, u3 = _inproj(xt, g1, wqkv, wut, qg, kg, bdq, bdk, layer)
        att = _attention(sink, q, k, v, bias, bsz, seq, layer)
        y3 = _ssm(u3, t_all, ms_all, mo_all, pw_all, dsk_all, layer, chunks_per_seq)
        xt = _outffn(xt, att, y3, wglu, wout, g2, w1, w2, layer)
    return xt.reshape(bsz, seq, D_MODEL)
```

```python
import functools
import math

import jax
import jax.numpy as jnp
from jax import lax
from jax.experimental import pallas as pl
from jax.experimental.pallas import tpu as pltpu

D_MODEL = 1024
N_HEADS = 8
N_KV = 2
GQA = N_HEADS // N_KV
HEAD_DIM = 64
ATT_W = N_HEADS * HEAD_DIM
KV_W = N_KV * HEAD_DIM
QKV_W = ATT_W + 2 * KV_W
WINDOW = 128
SSM_W = 512
SSM_CH = 16
SSM_G = SSM_W // SSM_CH
SSM_P = 64
D_FF = 4 * D_MODEL
EPS = 1e-6
LOG2E = math.log2(math.e)

LANES = 128
SUB = 32
N_SUB = LANES // SUB
CW = SSM_CH * SUB
VMEM_LIMIT = 56 * 1024 * 1024
STAGE_BYTES = 1024 * 1024

TM_IN = 1024
TM_HALF = 512
TM_FFN = 512
Y_CHUNKS = 8
ATT_NB = 4
SSM_GB = 2
PREP_GB = 4

_POW_N = (32, 64, 96, 128, 256, 512, 1024, 2048, 4096)
_POW_ROWS = 16
_N_SCAN_STEPS = 6

F32 = jnp.float32
BF16 = jnp.bfloat16
I32 = jnp.int32


def _dot(a, b):
    return jnp.dot(a, b, preferred_element_type=F32)


def _dot_nt(a, b, precision=None):
    return lax.dot_general(a, b, (((1,), (1,)), ((), ())), precision=precision,
                           preferred_element_type=F32)


def _prep_kernel(lam_ref, bt_ref, c_ref, t_ref, ms_ref, mo_ref, pw_ref,
                 g1r_ref, g1i_ref, mtr_ref, mti_ref):
    grp = range(lam_ref.shape[0])
    s_idx = lax.broadcasted_iota(I32, (SUB, LANES), 0)
    is_fwd = lax.broadcasted_iota(I32, (SUB, LANES), 1) < SSM_P
    fwd16 = lax.broadcasted_iota(I32, (SSM_CH, LANES), 1) < SSM_P
    e_t = jnp.where(is_fwd, s_idx, SUB - 1 - s_idx)

    def cmul(x, y):
        return x[0] * y[0] - x[1] * y[1], x[0] * y[1] + x[1] * y[0]

    def discretise(g):
        lr = lam_ref[g, 0:1, :]
        li = lam_ref[g, 1:2, :]
        dt = jnp.exp(lam_ref[g, 2:3, :])
        mag = jnp.exp(lr * dt)
        abr = mag * jnp.cos(li * dt)
        abi = mag * jnp.sin(li * dt)
        den = lr * lr + li * li
        zr = ((abr - 1.0) * lr + abi * li) / den
        zi = (abi * lr - (abr - 1.0) * li) / den
        btr = bt_ref[g, 0]
        bti = bt_ref[g, 1]
        sq = [(abr, abi)]
        for _ in range(max(_POW_N).bit_length() - 1):
            sq.append(cmul(sq[-1], sq[-1]))
        return sq, zr * btr - zi * bti, zr * bti + zi * btr

    def int_power(sq, e):
        acc = None
        for bit in range(SUB.bit_length() - 1):
            on = (e & (1 << bit)) != 0
            fac = (jnp.where(on, sq[bit][0], 1.0), jnp.where(on, sq[bit][1], 0.0))
            acc = fac if acc is None else cmul(acc, fac)
        return acc

    disc = [discretise(g) for g in grp]
    tpow = [int_power(disc[g][0], e_t) for g in grp]
    mspow = [int_power(disc[g][0], SUB - 1 - e_t) for g in grp]
    mopow = [cmul(tpow[g], disc[g][0][0]) for g in grp]

    for g in grp:
        _, bbr, bbi = disc[g]
        cr = c_ref[g, 0]
        ci = c_ref[g, 1]
        (tr, ti), (msr, msi), (mor, moi) = tpow[g], mspow[g], mopow[g]
        for h in range(SSM_CH):
            rows = slice(h * SUB, (h + 1) * SUB)
            b_r = bbr[h:h + 1, :]
            b_i = bbi[h:h + 1, :]
            ms_ref[g, rows, :LANES] = (msr * b_r - msi * b_i).astype(BF16)
            ms_ref[g, rows, LANES:] = (msr * b_i + msi * b_r).astype(BF16)
            c_r = cr[h:h + 1, :]
            c_i = ci[h:h + 1, :]
            g1r_ref[g, rows, :] = tr * c_r - ti * c_i
            g1i_ref[g, rows, :] = tr * c_i + ti * c_r
            mtr_ref[g, rows, :] = mor * c_r - moi * c_i
            mti_ref[g, rows, :] = -(mor * c_i + moi * c_r)
    for g in grp:
        mo_ref[g, :LANES, :] = mtr_ref[g].T.astype(BF16)
        mo_ref[g, LANES:, :] = mti_ref[g].T.astype(BF16)

    hp = lax.Precision.HIGHEST
    kts = []
    for g in grp:
        _, bbr, bbi = disc[g]
        g1r = g1r_ref[g]
        g1i = g1i_ref[g]
        kts.append([_dot_nt(jnp.where(sel, bbr, 0.0), g1r, precision=hp)
                    - _dot_nt(jnp.where(sel, bbi, 0.0), g1i, precision=hp)
                    for sel in (fwd16, jnp.logical_not(fwd16))])

    for g in grp:
        sq = disc[g][0]
        pw_ref[g] = jnp.zeros((2, _POW_ROWS, LANES), F32)
        for r, n in enumerate(_POW_N):
            acc = None
            for bit in range(n.bit_length()):
                if n >> bit & 1:
                    acc = sq[bit] if acc is None else cmul(acc, sq[bit])
            pw_ref[g, 0, r:r + 1, :] = acc[0]
            pw_ref[g, 1, r:r + 1, :] = acc[1]

    s_lane = lax.broadcasted_iota(I32, (SUB, CW), 1) & (SUB - 1)
    s_row = lax.broadcasted_iota(I32, (SUB, CW), 0)
    fwd_mask = s_lane >= s_row
    tau0 = (lax.broadcasted_iota(I32, (SSM_CH, CW), 1) & (SUB - 1)) == 0
    for g in grp:
        kfd = kts[g][0] + jnp.where(tau0, pltpu.roll(kts[g][1], CW - (SUB - 1), 1), 0.0)
        for h in range(SSM_CH):
            kf = jnp.broadcast_to(kfd[h:h + 1, :], (SUB, CW))
            kb = jnp.broadcast_to(kts[g][1][h:h + 1, :], (SUB, CW))
            rf = pltpu.roll(kf, 0, 1, stride=1, stride_axis=0)
            rb = pltpu.roll(kb, CW - (SUB - 1), 1, stride=1, stride_axis=0)
            t_ref[g, h * SUB:(h + 1) * SUB, :] = jnp.where(fwd_mask, rf, rb).astype(BF16)


def _ssm_prep(lam, bt, c):
    lg = lam.shape[0]
    sel3 = lambda i: (i, 0, 0)
    sel4 = lambda i: (i, 0, 0, 0)
    return pl.pallas_call(
        _prep_kernel,
        name="ssm_prep",
        grid=(lg // PREP_GB,),
        in_specs=[
            pl.BlockSpec((PREP_GB, 3, LANES), sel3),
            pl.BlockSpec((PREP_GB, 2, SSM_CH, LANES), sel4),
            pl.BlockSpec((PREP_GB, 2, SSM_CH, LANES), sel4),
        ],
        out_specs=[
            pl.BlockSpec((PREP_GB, CW, CW), sel3),
            pl.BlockSpec((PREP_GB, CW, 2 * LANES), sel3),
            pl.BlockSpec((PREP_GB, 2 * LANES, CW), sel3),
            pl.BlockSpec((PREP_GB, 2, _POW_ROWS, LANES), sel4),
        ],
        out_shape=[
            jax.ShapeDtypeStruct((lg, CW, CW), BF16),
            jax.ShapeDtypeStruct((lg, CW, 2 * LANES), BF16),
            jax.ShapeDtypeStruct((lg, 2 * LANES, CW), BF16),
            jax.ShapeDtypeStruct((lg, 2, _POW_ROWS, LANES), F32),
        ],
        scratch_shapes=[pltpu.VMEM((PREP_GB, CW, LANES), F32)] * 4,
        compiler_params=pltpu.CompilerParams(dimension_semantics=("arbitrary",)),
    )(lam, bt, c)


def _inproj_kernel(x_ref, g1_ref, win_ref, qg_ref, kg_ref, bdq_ref, bdk_ref,
                   q_ref, k_ref, v_ref, u_ref, wqkv_ref, wut_ref):
    @pl.when(pl.program_id(0) == 0)
    def _():
        wqkv_ref[...] = win_ref[:, :QKV_W].astype(BF16)
        wut_ref[...] = win_ref[:, QKV_W:].T.astype(BF16)

    def head_norm(t, bd_ref, gain):
        m = _dot((t * t).astype(BF16), bd_ref[...])
        return t * lax.rsqrt(m + EPS) * gain

    halves = [slice(r * TM_HALF, (r + 1) * TM_HALF) for r in range(TM_IN // TM_HALF)]
    hn = []
    for rows in halves:
        x = x_ref[rows, :]
        ms = jnp.mean(x * x, axis=-1, keepdims=True)
        hn.append((x * lax.rsqrt(ms + EPS) * g1_ref[...]).astype(BF16))
    z = [_dot(h, wqkv_ref[...]) for h in hn]
    ut = [_dot_nt(wut_ref[...], h) for h in hn]
    for rows, zz in zip(halves, z):
        q_ref[rows, :] = head_norm(zz[:, :ATT_W], bdq_ref, qg_ref[...]).astype(BF16)
        k_ref[rows, :] = head_norm(zz[:, ATT_W:ATT_W + KV_W], bdk_ref, kg_ref[...]).astype(BF16)
        v_ref[rows, :] = zz[:, ATT_W + KV_W:].astype(BF16)
    chunks = TM_IN // LANES
    u2d = u_ref.reshape(SSM_W * chunks, LANES)
    for r, utr in enumerate(ut):
        for jj in range(TM_HALF // LANES):
            j = r * (TM_HALF // LANES) + jj
            u2d[pl.ds(j, SSM_W, stride=chunks), :] = utr[:, jj * LANES:(jj + 1) * LANES]


def _inproj(x, g1, w_in, qg, kg, bdq, bdk, layer):
    t = x.shape[0]
    const = lambda i: (0, 0)
    sel = lambda i: (layer, 0, 0)
    return pl.pallas_call(
        _inproj_kernel,
        name="inproj",
        grid=(t // TM_IN,),
        in_specs=[
            pl.BlockSpec((TM_IN, D_MODEL), lambda i: (i, 0)),
            pl.BlockSpec((None, 1, D_MODEL), sel),
            pl.BlockSpec((None, D_MODEL, QKV_W + SSM_W), sel, pipeline_mode=pl.Buffered(1)),
            pl.BlockSpec((None, 1, ATT_W), sel),
            pl.BlockSpec((None, 1, KV_W), sel),
            pl.BlockSpec((ATT_W, ATT_W), const),
            pl.BlockSpec((KV_W, KV_W), const),
        ],
        out_specs=[
            pl.BlockSpec((TM_IN, ATT_W), lambda i: (i, 0)),
            pl.BlockSpec((TM_IN, KV_W), lambda i: (i, 0)),
            pl.BlockSpec((TM_IN, KV_W), lambda i: (i, 0)),
            pl.BlockSpec((SSM_W, TM_IN // LANES, LANES), lambda i: (0, i, 0)),
        ],
        out_shape=[
            jax.ShapeDtypeStruct((t, ATT_W), BF16),
            jax.ShapeDtypeStruct((t, KV_W), BF16),
            jax.ShapeDtypeStruct((t, KV_W), BF16),
            jax.ShapeDtypeStruct((SSM_W, t // LANES, LANES), F32),
        ],
        scratch_shapes=[pltpu.VMEM((D_MODEL, QKV_W), BF16), pltpu.VMEM((SSM_W, D_MODEL), BF16)],
        compiler_params=pltpu.CompilerParams(dimension_semantics=("arbitrary",),
                                             vmem_limit_bytes=VMEM_LIMIT),
    )(x, g1, w_in, qg, kg, bdq, bdk)


def _attn_kernel(sink_ref, q_ref, k_ref, v_ref, bias_ref, o_ref, s_scr, p_scr, *, n_blocks):
    n0 = pl.program_id(1) * ATT_NB
    prev = pl.multiple_of(jnp.maximum(n0 - 1, 0) * WINDOW, WINDOW)
    mid = pl.multiple_of(n0 * WINDOW, WINDOW)
    nxt = pl.multiple_of(jnp.minimum(n0 + ATT_NB, n_blocks - 1) * WINDOW, WINDOW)

    def window(ref):
        return jnp.concatenate([ref[0, pl.ds(prev, WINDOW), :], ref[0, pl.ds(mid, ATT_NB * WINDOW), :],
                                ref[0, pl.ds(nxt, WINDOW), :]], axis=0)

    kw = window(k_ref)
    vw = window(v_ref)
    kj = [kw[:, j * HEAD_DIM:(j + 1) * HEAD_DIM] for j in range(N_KV)]
    vj = [vw[:, j * HEAD_DIM:(j + 1) * HEAD_DIM] for j in range(N_KV)]

    def scores(b):
        n = n0 + b
        var = jnp.where(n == 0, 0, jnp.where(n == n_blocks - 1, 2, 1))
        qb = q_ref[b * WINDOW:(b + 1) * WINDOW, :]
        for h in range(N_HEADS):
            kh = kj[h // GQA][b * WINDOW:(b + 3) * WINDOW, :]
            s_scr[b % 2, h] = _dot_nt(qb[:, h * HEAD_DIM:(h + 1) * HEAD_DIM], kh) + bias_ref[var, h]

    def softmax(b):
        inv = []
        for h in range(N_HEADS):
            s = s_scr[b % 2, h]
            sk = sink_ref[h] * LOG2E
            m = jnp.maximum(jnp.max(s, axis=-1, keepdims=True), sk)
            p = jnp.exp2(s - m)
            inv.append(1.0 / (jnp.sum(p, axis=-1, keepdims=True) + jnp.exp2(sk - m)))
            p_scr[b % 2, h] = p.astype(BF16)
        return inv

    def weighted(b, inv):
        outs = []
        for h in range(N_HEADS):
            vh = vj[h // GQA][b * WINDOW:(b + 3) * WINDOW, :]
            outs.append(_dot(p_scr[b % 2, h], vh) * inv[h])
        o_ref[b * WINDOW:(b + 1) * WINDOW, :] = jnp.concatenate(outs, axis=-1).astype(BF16)

    scores(0)
    for b in range(ATT_NB):
        if b + 1 < ATT_NB:
            scores(b + 1)
        weighted(b, softmax(b))


def _attention(sink, q, k, v, bias, bsz, seq, layer):
    n_blocks = seq // WINDOW
    steps = n_blocks // ATT_NB
    rows = ATT_NB * WINDOW
    return pl.pallas_call(
        functools.partial(_attn_kernel, n_blocks=n_blocks),
        name="band_attn",
        grid=(bsz, steps),
        in_specs=[
            pl.BlockSpec(memory_space=pltpu.SMEM),
            pl.BlockSpec((rows, ATT_W), lambda b, n: (b * steps + n, 0)),
            pl.BlockSpec((1, seq, KV_W), lambda b, n: (b, 0, 0)),
            pl.BlockSpec((1, seq, KV_W), lambda b, n: (b, 0, 0)),
            pl.BlockSpec((3, N_HEADS, WINDOW, 3 * WINDOW), lambda b, n: (0, 0, 0, 0)),
        ],
        out_specs=pl.BlockSpec((rows, ATT_W), lambda b, n: (b * steps + n, 0)),
        out_shape=jax.ShapeDtypeStruct((bsz * seq, ATT_W), BF16),
        scratch_shapes=[pltpu.VMEM((2, N_HEADS, WINDOW, 3 * WINDOW), F32),
                        pltpu.VMEM((2, N_HEADS, WINDOW, 3 * WINDOW), BF16)],
        compiler_params=pltpu.CompilerParams(dimension_semantics=("arbitrary", "arbitrary"),
                                             vmem_limit_bytes=VMEM_LIMIT),
    )(sink[layer], q, k.reshape(bsz, seq, KV_W), v.reshape(bsz, seq, KV_W), bias)


def _attn_bias(seq):
    q_idx = jnp.arange(WINDOW)[:, None]
    c_idx = jnp.arange(3 * WINDOW)[None, :]
    dist = jnp.abs(q_idx - c_idx + WINDOW)
    slopes = jnp.exp2(-8.0 * jnp.arange(1, N_HEADS + 1, dtype=F32) / N_HEADS)
    alibi = -slopes[:, None, None] * dist.astype(F32)[None]
    in_band = (dist <= WINDOW)[None]
    blk = (c_idx // WINDOW)[None]
    variants = []
    for bad in (0, -1, 2):
        ok = in_band & (blk != bad)
        variants.append(jnp.where(ok, alibi * LOG2E, -1e30))
    return jnp.stack(variants).astype(F32)


def _swap_lane_blocks(vs):
    lane = lax.broadcasted_iota(I32, vs[0].shape, 1)
    lo_half = lane < 2 * SUB
    even_blk = (lane & SUB) == 0
    a0, a1, a2, a3 = vs
    c0 = jnp.where(lo_half, a0, pltpu.roll(a2, 2 * SUB, 1))
    c2 = jnp.where(lo_half, pltpu.roll(a0, 2 * SUB, 1), a2)
    c1 = jnp.where(lo_half, a1, pltpu.roll(a3, 2 * SUB, 1))
    c3 = jnp.where(lo_half, pltpu.roll(a1, 2 * SUB, 1), a3)
    b0 = jnp.where(even_blk, c0, pltpu.roll(c1, SUB, 1))
    b1 = jnp.where(even_blk, pltpu.roll(c0, LANES - SUB, 1), c1)
    b2 = jnp.where(even_blk, c2, pltpu.roll(c3, SUB, 1))
    b3 = jnp.where(even_blk, pltpu.roll(c2, LANES - SUB, 1), c3)
    return [b0, b1, b2, b3]


def _ssm_kernel(u_ref, t_ref, ms_ref, mo_ref, pw_ref, dsk_ref, y_ref, *, chunks_per_seq):
    n_grp = u_ref.shape[0] // SSM_CH
    nbc = u_ref.shape[1]
    is_fwd = lax.broadcasted_iota(I32, (nbc, LANES), 1) < SSM_P
    c_idx = lax.broadcasted_iota(I32, (nbc, LANES), 0) & (chunks_per_seq - 1)

    def cmul(x, a):
        return a[0] * x[0] - a[1] * x[1], a[0] * x[1] + a[1] * x[0]

    def cadd(x, y):
        return x[0] + y[0], x[1] + y[1]

    def shifted(x, dd):
        ok_f = jnp.logical_and(is_fwd, c_idx >= dd)
        ok_b = jnp.logical_and(jnp.logical_not(is_fwd), c_idx < chunks_per_seq - dd)
        return tuple(jnp.where(ok_f, pltpu.roll(t, dd, 0), jnp.where(ok_b, pltpu.roll(t, nbc - dd, 0), 0.0))
                     for t in x)

    def gather(g):
        slabs = [u_ref[g * SSM_CH + h] for h in range(SSM_CH)]
        cols = [_swap_lane_blocks(slabs[4 * q:4 * q + 4]) for q in range(SSM_CH // 4)]
        pieces = [jnp.concatenate([cols[q][i] for q in range(SSM_CH // 4)], axis=1)
                  for i in range(N_SUB)]
        return jnp.concatenate(pieces, axis=0)

    def entering_states(g, s_all):
        def prow(r):
            return pw_ref[g, 0, r:r + 1, :], pw_ref[g, 1, r:r + 1, :]

        def mirror(k):
            lo, hi = k * nbc, (N_SUB - 1 - k) * nbc
            return tuple(jnp.where(is_fwd, s_all[lo:lo + nbc, c:c + LANES], s_all[hi:hi + nbc, c:c + LANES])
                         for c in (0, LANES))

        a32 = prow(0)
        sk = [mirror(k) for k in range(N_SUB)]
        zero = jnp.zeros((nbc, LANES), F32)
        pk = [(zero, zero)]
        for k in range(N_SUB - 1):
            pk.append(cadd(cmul(pk[k], a32), sk[k]))
        x = cadd(cmul(pk[N_SUB - 1], a32), sk[N_SUB - 1])
        for kk in range(_N_SCAN_STEPS):
            x = cadd(x, cmul(shifted(x, 1 << kk), prow(3 + kk)))
        ent = shifted(x, 1)
        qk = [cadd(pk[k], cmul(ent, prow(k - 1)) if k > 0 else ent) for k in range(N_SUB)]
        xin = []
        for i in range(N_SUB):
            j = N_SUB - 1 - i
            xin.append(jnp.concatenate([jnp.where(is_fwd, qk[i][c], qk[j][c]) for c in (0, 1)], axis=1))
        return jnp.concatenate(xin, axis=0).astype(BF16)

    def scatter(g, act):
        for q in range(SSM_CH // 4):
            back = _swap_lane_blocks([act[i * nbc:(i + 1) * nbc, q * LANES:(q + 1) * LANES]
                                      for i in range(N_SUB)])
            for r in range(4):
                y_ref[g * SSM_CH + 4 * q + r] = back[r]

    grp = range(n_grp)
    ust = [gather(g) for g in grp]
    ub = [u.astype(BF16) for u in ust]
    z = [_dot(ub[g], t_ref[g]) for g in grp]
    s_all = [_dot(ub[g], ms_ref[g]) for g in grp]
    xin = [entering_states(g, s_all[g]) for g in grp]
    for g in grp:
        y = z[g] + _dot(xin[g], mo_ref[g]) + dsk_ref[g] * ust[g]
        scatter(g, jax.nn.gelu(y))


def _ssm(u3, t_all, ms_all, mo_all, pw_all, dsk_all, layer, chunks_per_seq):
    nbc = u3.shape[1]
    base = layer * (SSM_G // SSM_GB)
    sel3 = lambda g: (base + g, 0, 0)
    slab_spec = pl.BlockSpec((SSM_GB * SSM_CH, nbc, LANES), lambda g: (g, 0, 0))
    return pl.pallas_call(
        functools.partial(_ssm_kernel, chunks_per_seq=chunks_per_seq),
        name="ssm_mix",
        grid=(SSM_G // SSM_GB,),
        in_specs=[
            slab_spec,
            pl.BlockSpec((SSM_GB, CW, CW), sel3),
            pl.BlockSpec((SSM_GB, CW, 2 * LANES), sel3),
            pl.BlockSpec((SSM_GB, 2 * LANES, CW), sel3),
            pl.BlockSpec((SSM_GB, 2, _POW_ROWS, LANES), lambda g: (base + g, 0, 0, 0)),
            pl.BlockSpec((SSM_GB, 1, CW), sel3),
        ],
        out_specs=slab_spec,
        out_shape=jax.ShapeDtypeStruct((SSM_W, nbc, LANES), F32),
        compiler_params=pltpu.CompilerParams(dimension_semantics=("arbitrary",),
                                             vmem_limit_bytes=VMEM_LIMIT),
    )(u3, t_all, ms_all, mo_all, pw_all, dsk_all)


def _stream_cast(src_hbm, layer, dst_ref, stage_ref, sem_ref):
    chunk = stage_ref.shape[1]
    n_chunks = dst_ref.shape[0] // chunk

    def copy(c):
        return pltpu.make_async_copy(src_hbm.at[layer, pl.ds(c * chunk, chunk), :],
                                     stage_ref.at[c % 2], sem_ref.at[c % 2])

    copy(0).start()
    for c in range(n_chunks):
        if c + 1 < n_chunks:
            copy(c + 1).start()
        copy(c).wait()
        dst_ref[c * chunk:(c + 1) * chunk, :] = stage_ref[c % 2].astype(BF16)


def _outffn_kernel(x_ref, att_ref, y_ref, g2_ref, wglu_hbm, wout_hbm, w1_hbm, w2_hbm, o_ref,
                   wglu_ref, wout_ref, w1_ref, w2_ref, wide_stage, tall_stage, wide_sem, tall_sem,
                   *, layer):
    @pl.when(pl.program_id(0) == 0)
    def _():
        _stream_cast(wglu_hbm, layer, wglu_ref, tall_stage, tall_sem)
        _stream_cast(wout_hbm, layer, wout_ref, tall_stage, tall_sem)
        _stream_cast(w1_hbm, layer, w1_ref, wide_stage, wide_sem)
        _stream_cast(w2_hbm, layer, w2_ref, tall_stage, tall_sem)

    first = (pl.program_id(0) % (Y_CHUNKS * LANES // TM_FFN)) * (TM_FFN // LANES)
    y2d = y_ref.reshape(SSM_W * Y_CHUNKS, LANES)
    ys = [y2d[pl.ds(first + j, SSM_W, stride=Y_CHUNKS), :].T
          for j in range(TM_FFN // LANES)]
    y = jnp.concatenate(ys, axis=0).astype(BF16)
    gl = _dot(y, wglu_ref[...])
    ssm = gl[:, :SSM_W] * jax.nn.sigmoid(gl[:, SSM_W:])
    x = x_ref[...] + _dot(att_ref[...], wout_ref[:ATT_W, :]) \
        + _dot(ssm.astype(BF16), wout_ref[ATT_W:, :])
    ms = jnp.mean(x * x, axis=-1, keepdims=True)
    hn = (x * lax.rsqrt(ms + EPS) * g2_ref[...]).astype(BF16)
    hid = jnp.maximum(_dot(hn, w1_ref[...]), 0.0)
    o_ref[...] = x + _dot((hid * hid).astype(BF16), w2_ref[...])


def _outffn(x, att, y3, g2, w_glu, w_out, w_ff1, w_ff2, layer):
    t = x.shape[0]
    hbm = pl.BlockSpec(memory_space=pl.ANY)
    wide_rows = STAGE_BYTES // (4 * D_FF)
    tall_rows = STAGE_BYTES // (4 * D_MODEL)
    return pl.pallas_call(
        functools.partial(_outffn_kernel, layer=layer),
        name="out_ffn",
        grid=(t // TM_FFN,),
        in_specs=[
            pl.BlockSpec((TM_FFN, D_MODEL), lambda i: (i, 0)),
            pl.BlockSpec((TM_FFN, ATT_W), lambda i: (i, 0)),
            pl.BlockSpec((SSM_W, Y_CHUNKS, LANES), lambda i: (0, i * TM_FFN // (Y_CHUNKS * LANES), 0)),
            pl.BlockSpec((None, 1, D_MODEL), lambda i: (layer, 0, 0)),
            hbm, hbm, hbm, hbm,
        ],
        out_specs=pl.BlockSpec((TM_FFN, D_MODEL), lambda i: (i, 0)),
        out_shape=jax.ShapeDtypeStruct((t, D_MODEL), F32),
        scratch_shapes=[
            pltpu.VMEM((SSM_W, 2 * SSM_W), BF16),
            pltpu.VMEM((D_MODEL, D_MODEL), BF16),
            pltpu.VMEM((D_MODEL, D_FF), BF16),
            pltpu.VMEM((D_FF, D_MODEL), BF16),
            pltpu.VMEM((2, wide_rows, D_FF), F32),
            pltpu.VMEM((2, tall_rows, D_MODEL), F32),
            pltpu.SemaphoreType.DMA((2,)),
            pltpu.SemaphoreType.DMA((2,)),
        ],
        compiler_params=pltpu.CompilerParams(dimension_semantics=("arbitrary",),
                                             vmem_limit_bytes=VMEM_LIMIT),
    )(x, att, y3, g2, w_glu, w_out, w_ff1, w_ff2)


def _block_diag_mean(width):
    idx = jnp.arange(width) // HEAD_DIM
    return jnp.where(idx[:, None] == idx[None, :], 1.0 / HEAD_DIM, 0.0).astype(BF16)


def _prep_operands(lam_re, lam_im, log_dt, b_re, b_im, c_re, c_im, d_skip):
    depth = lam_re.shape[0]
    lg = depth * SSM_G
    both = lambda a: jnp.concatenate([a[:, 0], a[:, 1]], axis=-1)
    ldt = jnp.broadcast_to(log_dt[..., None], lam_re.shape)
    lam = jnp.stack([both(lam_re), both(lam_im), both(ldt)], axis=2).reshape(lg, 3, LANES)
    bt = jnp.stack([b_re, b_im], axis=2).transpose(0, 1, 2, 4, 3)
    bt = jnp.concatenate([bt, bt], axis=-1).reshape(lg, 2, SSM_CH, LANES)
    cc = jnp.stack([both(c_re), both(c_im)], axis=2).reshape(lg, 2, SSM_CH, LANES)
    dsk = jnp.repeat(d_skip.reshape(lg, 1, SSM_CH), SUB, axis=-1)
    return list(_ssm_prep(lam, bt, cc)) + [dsk]


def kernel(x, norm1, w_in, q_gain, k_gain, sink, lam_re, lam_im, log_dt, b_re, b_im, c_re, c_im,
           d_skip, w_glu, w_out, norm2, w_ff1, w_ff2):
    bsz, seq, _ = x.shape
    depth = norm1.shape[0]
    tokens = bsz * seq
    chunks_per_seq = seq // LANES
    assert seq % (ATT_NB * WINDOW) == 0 and tokens % TM_IN == 0 and tokens % TM_FFN == 0
    assert TM_IN == Y_CHUNKS * LANES and (Y_CHUNKS * LANES) % TM_FFN == 0
    assert chunks_per_seq == 1 << _N_SCAN_STEPS and max(_POW_N) == LANES * chunks_per_seq // 2

    t_all, ms_all, mo_all, pw_all, dsk_all = _prep_operands(lam_re, lam_im, log_dt, b_re, b_im,
                                                           c_re, c_im, d_skip)
    g1 = norm1[:, None, :]
    g2 = norm2[:, None, :]
    qg = (jnp.tile(q_gain, (1, N_HEADS)) * (LOG2E / math.sqrt(HEAD_DIM)))[:, None, :]
    kg = jnp.tile(k_gain, (1, N_KV))[:, None, :]
    bias = _attn_bias(seq)
    bdq = _block_diag_mean(ATT_W)
    bdk = _block_diag_mean(KV_W)

    xt = x.reshape(tokens, D_MODEL)
    for layer in range(depth):
        q, k, v, u3 = _inproj(xt, g1, w_in, qg, kg, bdq, bdk, layer)
        att = _attention(sink, q, k, v, bias, bsz, seq, layer)
        y3 = _ssm(u3, t_all, ms_all, mo_all, pw_all, dsk_all, layer, chunks_per_seq)
        xt = _outffn(xt, att, y3, g2, w_glu, w_out, w_ff1, w_ff2, layer)
    return xt.reshape(bsz, seq, D_MODEL)
```

```python
import functools
import math

import jax
import jax.numpy as jnp
from jax import lax
from jax.experimental import pallas as pl
from jax.experimental.pallas import tpu as pltpu

D_MODEL = 1024
N_HEADS = 8
N_KV = 2
GQA = N_HEADS // N_KV
HEAD_DIM = 64
ATT_W = N_HEADS * HEAD_DIM
KV_W = N_KV * HEAD_DIM
QKV_W = ATT_W + 2 * KV_W
WINDOW = 128
SSM_W = 512
SSM_CH = 16
SSM_G = SSM_W // SSM_CH
SSM_P = 64
D_FF = 4 * D_MODEL
EPS = 1e-6
LOG2E = math.log2(math.e)

LANES = 128
SUB = 32
N_SUB = LANES // SUB
CW = SSM_CH * SUB
VMEM_LIMIT = 56 * 1024 * 1024
STAGE_ROWS = 256
STAGE_SLOTS = 8

TM_IN = 1024
TM_HALF = 512
TM_FFN = 512
Y_CHUNKS = 8
ATT_NB = 4
SSM_GB = 2
PREP_GB = 4

_POW_N = (32, 64, 96, 128, 256, 512, 1024, 2048, 4096)
_POW_ROWS = 16
_N_SCAN_STEPS = 6

F32 = jnp.float32
BF16 = jnp.bfloat16
I32 = jnp.int32


def _dot(a, b):
    return jnp.dot(a, b, preferred_element_type=F32)


def _dot_nt(a, b, precision=None):
    return lax.dot_general(a, b, (((1,), (1,)), ((), ())), precision=precision,
                           preferred_element_type=F32)


def _prep_kernel(lam_ref, bt_ref, c_ref, t_ref, ms_ref, mo_ref, pw_ref,
                 g1r_ref, g1i_ref, mtr_ref, mti_ref):
    grp = range(lam_ref.shape[0])
    s_idx = lax.broadcasted_iota(I32, (SUB, LANES), 0)
    is_fwd = lax.broadcasted_iota(I32, (SUB, LANES), 1) < SSM_P
    fwd16 = lax.broadcasted_iota(I32, (SSM_CH, LANES), 1) < SSM_P
    e_t = jnp.where(is_fwd, s_idx, SUB - 1 - s_idx)

    def cmul(x, y):
        return x[0] * y[0] - x[1] * y[1], x[0] * y[1] + x[1] * y[0]

    def discretise(g):
        lr = lam_ref[g, 0:1, :]
        li = lam_ref[g, 1:2, :]
        dt = jnp.exp(lam_ref[g, 2:3, :])
        mag = jnp.exp(lr * dt)
        abr = mag * jnp.cos(li * dt)
        abi = mag * jnp.sin(li * dt)
        den = lr * lr + li * li
        zr = ((abr - 1.0) * lr + abi * li) / den
        zi = (abi * lr - (abr - 1.0) * li) / den
        btr = bt_ref[g, 0]
        bti = bt_ref[g, 1]
        sq = [(abr, abi)]
        for _ in range(max(_POW_N).bit_length() - 1):
            sq.append(cmul(sq[-1], sq[-1]))
        return sq, zr * btr - zi * bti, zr * bti + zi * btr

    def int_power(sq, e):
        acc = None
        for bit in range(SUB.bit_length() - 1):
            on = (e & (1 << bit)) != 0
            fac = (jnp.where(on, sq[bit][0], 1.0), jnp.where(on, sq[bit][1], 0.0))
            acc = fac if acc is None else cmul(acc, fac)
        return acc

    disc = [discretise(g) for g in grp]
    tpow = [int_power(disc[g][0], e_t) for g in grp]
    mspow = [int_power(disc[g][0], SUB - 1 - e_t) for g in grp]
    mopow = [cmul(tpow[g], disc[g][0][0]) for g in grp]

    for g in grp:
        _, bbr, bbi = disc[g]
        cr = c_ref[g, 0]
        ci = c_ref[g, 1]
        (tr, ti), (msr, msi), (mor, moi) = tpow[g], mspow[g], mopow[g]
        for h in range(SSM_CH):
            rows = slice(h * SUB, (h + 1) * SUB)
            b_r = bbr[h:h + 1, :]
            b_i = bbi[h:h + 1, :]
            ms_ref[g, rows, :LANES] = (msr * b_r - msi * b_i).astype(BF16)
            ms_ref[g, rows, LANES:] = (msr * b_i + msi * b_r).astype(BF16)
            c_r = cr[h:h + 1, :]
            c_i = ci[h:h + 1, :]
            g1r_ref[g, rows, :] = tr * c_r - ti * c_i
            g1i_ref[g, rows, :] = tr * c_i + ti * c_r
            mtr_ref[g, rows, :] = mor * c_r - moi * c_i
            mti_ref[g, rows, :] = -(mor * c_i + moi * c_r)
    for g in grp:
        mo_ref[g, :LANES, :] = mtr_ref[g].T.astype(BF16)
        mo_ref[g, LANES:, :] = mti_ref[g].T.astype(BF16)

    hp = lax.Precision.HIGHEST
    kts = []
    for g in grp:
        _, bbr, bbi = disc[g]
        g1r = g1r_ref[g]
        g1i = g1i_ref[g]
        kts.append([_dot_nt(jnp.where(sel, bbr, 0.0), g1r, precision=hp)
                    - _dot_nt(jnp.where(sel, bbi, 0.0), g1i, precision=hp)
                    for sel in (fwd16, jnp.logical_not(fwd16))])

    for g in grp:
        sq = disc[g][0]
        pw_ref[g] = jnp.zeros((2, _POW_ROWS, LANES), F32)
        for r, n in enumerate(_POW_N):
            acc = None
            for bit in range(n.bit_length()):
                if n >> bit & 1:
                    acc = sq[bit] if acc is None else cmul(acc, sq[bit])
            pw_ref[g, 0, r:r + 1, :] = acc[0]
            pw_ref[g, 1, r:r + 1, :] = acc[1]

    s_lane = lax.broadcasted_iota(I32, (SUB, CW), 1) & (SUB - 1)
    s_row = lax.broadcasted_iota(I32, (SUB, CW), 0)
    fwd_mask = s_lane >= s_row
    tau0 = (lax.broadcasted_iota(I32, (SSM_CH, CW), 1) & (SUB - 1)) == 0
    for g in grp:
        kfd = kts[g][0] + jnp.where(tau0, pltpu.roll(kts[g][1], CW - (SUB - 1), 1), 0.0)
        for h in range(SSM_CH):
            kf = jnp.broadcast_to(kfd[h:h + 1, :], (SUB, CW))
            kb = jnp.broadcast_to(kts[g][1][h:h + 1, :], (SUB, CW))
            rf = pltpu.roll(kf, 0, 1, stride=1, stride_axis=0)
            rb = pltpu.roll(kb, CW - (SUB - 1), 1, stride=1, stride_axis=0)
            t_ref[g, h * SUB:(h + 1) * SUB, :] = jnp.where(fwd_mask, rf, rb).astype(BF16)


def _ssm_prep(lam, bt, c):
    lg = lam.shape[0]
    sel3 = lambda i: (i, 0, 0)
    sel4 = lambda i: (i, 0, 0, 0)
    return pl.pallas_call(
        _prep_kernel,
        name="ssm_prep",
        grid=(lg // PREP_GB,),
        in_specs=[
            pl.BlockSpec((PREP_GB, 3, LANES), sel3),
            pl.BlockSpec((PREP_GB, 2, SSM_CH, LANES), sel4),
            pl.BlockSpec((PREP_GB, 2, SSM_CH, LANES), sel4),
        ],
        out_specs=[
            pl.BlockSpec((PREP_GB, CW, CW), sel3),
            pl.BlockSpec((PREP_GB, CW, 2 * LANES), sel3),
            pl.BlockSpec((PREP_GB, 2 * LANES, CW), sel3),
            pl.BlockSpec((PREP_GB, 2, _POW_ROWS, LANES), sel4),
        ],
        out_shape=[
            jax.ShapeDtypeStruct((lg, CW, CW), BF16),
            jax.ShapeDtypeStruct((lg, CW, 2 * LANES), BF16),
            jax.ShapeDtypeStruct((lg, 2 * LANES, CW), BF16),
            jax.ShapeDtypeStruct((lg, 2, _POW_ROWS, LANES), F32),
        ],
        scratch_shapes=[pltpu.VMEM((PREP_GB, CW, LANES), F32)] * 4,
        compiler_params=pltpu.CompilerParams(dimension_semantics=("arbitrary",)),
    )(lam, bt, c)


def _inproj_kernel(x_ref, g1_ref, win_ref, qg_ref, kg_ref, bdq_ref, bdk_ref,
                   q_ref, k_ref, v_ref, u_ref, wqkv_ref, wut_ref):
    @pl.when(pl.program_id(0) == 0)
    def _():
        wqkv_ref[...] = win_ref[:, :QKV_W].astype(BF16)
        wut_ref[...] = win_ref[:, QKV_W:].T.astype(BF16)

    def head_norm(t, bd_ref, gain):
        m = _dot((t * t).astype(BF16), bd_ref[...])
        return t * lax.rsqrt(m + EPS) * gain

    halves = [slice(r * TM_HALF, (r + 1) * TM_HALF) for r in range(TM_IN // TM_HALF)]
    hn = []
    for rows in halves:
        x = x_ref[rows, :]
        ms = jnp.mean(x * x, axis=-1, keepdims=True)
        hn.append((x * lax.rsqrt(ms + EPS) * g1_ref[...]).astype(BF16))
    z = [_dot(h, wqkv_ref[...]) for h in hn]
    ut = [_dot_nt(wut_ref[...], h) for h in hn]
    for rows, zz in zip(halves, z):
        q_ref[rows, :] = head_norm(zz[:, :ATT_W], bdq_ref, qg_ref[...]).astype(BF16)
        k_ref[rows, :] = head_norm(zz[:, ATT_W:ATT_W + KV_W], bdk_ref, kg_ref[...]).astype(BF16)
        v_ref[rows, :] = zz[:, ATT_W + KV_W:].astype(BF16)
    chunks = TM_IN // LANES
    u2d = u_ref.reshape(SSM_W * chunks, LANES)
    for r, utr in enumerate(ut):
        for jj in range(TM_HALF // LANES):
            j = r * (TM_HALF // LANES) + jj
            u2d[pl.ds(j, SSM_W, stride=chunks), :] = utr[:, jj * LANES:(jj + 1) * LANES]


def _inproj(x, g1, w_in, qg, kg, bdq, bdk, layer):
    t = x.shape[0]
    const = lambda i: (0, 0)
    sel = lambda i: (layer, 0, 0)
    return pl.pallas_call(
        _inproj_kernel,
        name="inproj",
        grid=(t // TM_IN,),
        in_specs=[
            pl.BlockSpec((TM_IN, D_MODEL), lambda i: (i, 0)),
            pl.BlockSpec((None, 1, D_MODEL), sel),
            pl.BlockSpec((None, D_MODEL, QKV_W + SSM_W), sel, pipeline_mode=pl.Buffered(1)),
            pl.BlockSpec((None, 1, ATT_W), sel),
            pl.BlockSpec((None, 1, KV_W), sel),
            pl.BlockSpec((ATT_W, ATT_W), const),
            pl.BlockSpec((KV_W, KV_W), const),
        ],
        out_specs=[
            pl.BlockSpec((TM_IN, ATT_W), lambda i: (i, 0)),
            pl.BlockSpec((TM_IN, KV_W), lambda i: (i, 0)),
            pl.BlockSpec((TM_IN, KV_W), lambda i: (i, 0)),
            pl.BlockSpec((SSM_W, TM_IN // LANES, LANES), lambda i: (0, i, 0)),
        ],
        out_shape=[
            jax.ShapeDtypeStruct((t, ATT_W), BF16),
            jax.ShapeDtypeStruct((t, KV_W), BF16),
            jax.ShapeDtypeStruct((t, KV_W), BF16),
            jax.ShapeDtypeStruct((SSM_W, t // LANES, LANES), F32),
        ],
        scratch_shapes=[pltpu.VMEM((D_MODEL, QKV_W), BF16), pltpu.VMEM((SSM_W, D_MODEL), BF16)],
        compiler_params=pltpu.CompilerParams(dimension_semantics=("arbitrary",),
                                             vmem_limit_bytes=VMEM_LIMIT),
    )(x, g1, w_in, qg, kg, bdq, bdk)


def _attn_kernel(sink_ref, q_ref, k_ref, v_ref, bias_ref, o_ref, s_scr, p_scr, *, n_blocks):
    n0 = pl.program_id(1) * ATT_NB
    prev = pl.multiple_of(jnp.maximum(n0 - 1, 0) * WINDOW, WINDOW)
    mid = pl.multiple_of(n0 * WINDOW, WINDOW)
    nxt = pl.multiple_of(jnp.minimum(n0 + ATT_NB, n_blocks - 1) * WINDOW, WINDOW)

    def window(ref):
        return jnp.concatenate([ref[0, pl.ds(prev, WINDOW), :], ref[0, pl.ds(mid, ATT_NB * WINDOW), :],
                                ref[0, pl.ds(nxt, WINDOW), :]], axis=0)

    kw = window(k_ref)
    vw = window(v_ref)
    kj = [kw[:, j * HEAD_DIM:(j + 1) * HEAD_DIM] for j in range(N_KV)]
    vj = [vw[:, j * HEAD_DIM:(j + 1) * HEAD_DIM] for j in range(N_KV)]

    def scores(b):
        n = n0 + b
        var = jnp.where(n == 0, 0, jnp.where(n == n_blocks - 1, 2, 1))
        qb = q_ref[b * WINDOW:(b + 1) * WINDOW, :]
        for h in range(N_HEADS):
            kh = kj[h // GQA][b * WINDOW:(b + 3) * WINDOW, :]
            s_scr[b % 2, h] = _dot_nt(qb[:, h * HEAD_DIM:(h + 1) * HEAD_DIM], kh) + bias_ref[var, h]

    def softmax(b):
        inv = []
        for h in range(N_HEADS):
            s = s_scr[b % 2, h]
            sk = sink_ref[h] * LOG2E
            m = jnp.maximum(jnp.max(s, axis=-1, keepdims=True), sk)
            p = jnp.exp2(s - m)
            inv.append(1.0 / (jnp.sum(p, axis=-1, keepdims=True) + jnp.exp2(sk - m)))
            p_scr[b % 2, h] = p.astype(BF16)
        return inv

    def weighted(b, inv):
        outs = []
        for h in range(N_HEADS):
            vh = vj[h // GQA][b * WINDOW:(b + 3) * WINDOW, :]
            outs.append(_dot(p_scr[b % 2, h], vh) * inv[h])
        o_ref[b * WINDOW:(b + 1) * WINDOW, :] = jnp.concatenate(outs, axis=-1).astype(BF16)

    scores(0)
    for b in range(ATT_NB):
        if b + 1 < ATT_NB:
            scores(b + 1)
        weighted(b, softmax(b))


def _attention(sink, q, k, v, bias, bsz, seq, layer):
    n_blocks = seq // WINDOW
    steps = n_blocks // ATT_NB
    rows = ATT_NB * WINDOW
    return pl.pallas_call(
        functools.partial(_attn_kernel, n_blocks=n_blocks),
        name="band_attn",
        grid=(bsz, steps),
        in_specs=[
            pl.BlockSpec(memory_space=pltpu.SMEM),
            pl.BlockSpec((rows, ATT_W), lambda b, n: (b * steps + n, 0)),
            pl.BlockSpec((1, seq, KV_W), lambda b, n: (b, 0, 0)),
            pl.BlockSpec((1, seq, KV_W), lambda b, n: (b, 0, 0)),
            pl.BlockSpec((3, N_HEADS, WINDOW, 3 * WINDOW), lambda b, n: (0, 0, 0, 0)),
        ],
        out_specs=pl.BlockSpec((rows, ATT_W), lambda b, n: (b * steps + n, 0)),
        out_shape=jax.ShapeDtypeStruct((bsz * seq, ATT_W), BF16),
        scratch_shapes=[pltpu.VMEM((2, N_HEADS, WINDOW, 3 * WINDOW), F32),
                        pltpu.VMEM((2, N_HEADS, WINDOW, 3 * WINDOW), BF16)],
        compiler_params=pltpu.CompilerParams(dimension_semantics=("arbitrary", "arbitrary"),
                                             vmem_limit_bytes=VMEM_LIMIT),
    )(sink[layer], q, k.reshape(bsz, seq, KV_W), v.reshape(bsz, seq, KV_W), bias)


def _attn_bias(seq):
    q_idx = jnp.arange(WINDOW)[:, None]
    c_idx = jnp.arange(3 * WINDOW)[None, :]
    dist = jnp.abs(q_idx - c_idx + WINDOW)
    slopes = jnp.exp2(-8.0 * jnp.arange(1, N_HEADS + 1, dtype=F32) / N_HEADS)
    alibi = -slopes[:, None, None] * dist.astype(F32)[None]
    in_band = (dist <= WINDOW)[None]
    blk = (c_idx // WINDOW)[None]
    variants = []
    for bad in (0, -1, 2):
        ok = in_band & (blk != bad)
        variants.append(jnp.where(ok, alibi * LOG2E, -1e30))
    return jnp.stack(variants).astype(F32)


def _swap_lane_blocks(vs):
    lane = lax.broadcasted_iota(I32, vs[0].shape, 1)
    lo_half = lane < 2 * SUB
    even_blk = (lane & SUB) == 0
    a0, a1, a2, a3 = vs
    c0 = jnp.where(lo_half, a0, pltpu.roll(a2, 2 * SUB, 1))
    c2 = jnp.where(lo_half, pltpu.roll(a0, 2 * SUB, 1), a2)
    c1 = jnp.where(lo_half, a1, pltpu.roll(a3, 2 * SUB, 1))
    c3 = jnp.where(lo_half, pltpu.roll(a1, 2 * SUB, 1), a3)
    b0 = jnp.where(even_blk, c0, pltpu.roll(c1, SUB, 1))
    b1 = jnp.where(even_blk, pltpu.roll(c0, LANES - SUB, 1), c1)
    b2 = jnp.where(even_blk, c2, pltpu.roll(c3, SUB, 1))
    b3 = jnp.where(even_blk, pltpu.roll(c2, LANES - SUB, 1), c3)
    return [b0, b1, b2, b3]


def _ssm_kernel(u_ref, t_ref, ms_ref, mo_ref, pw_ref, dsk_ref, y_ref, *, chunks_per_seq):
    n_grp = u_ref.shape[0] // SSM_CH
    nbc = u_ref.shape[1]
    is_fwd = lax.broadcasted_iota(I32, (nbc, LANES), 1) < SSM_P
    c_idx = lax.broadcasted_iota(I32, (nbc, LANES), 0) & (chunks_per_seq - 1)

    def cmul(x, a):
        return a[0] * x[0] - a[1] * x[1], a[0] * x[1] + a[1] * x[0]

    def cadd(x, y):
        return x[0] + y[0], x[1] + y[1]

    def shifted(x, dd):
        ok_f = jnp.logical_and(is_fwd, c_idx >= dd)
        ok_b = jnp.logical_and(jnp.logical_not(is_fwd), c_idx < chunks_per_seq - dd)
        return tuple(jnp.where(ok_f, pltpu.roll(t, dd, 0), jnp.where(ok_b, pltpu.roll(t, nbc - dd, 0), 0.0))
                     for t in x)

    def gather(g):
        slabs = [u_ref[g * SSM_CH + h] for h in range(SSM_CH)]
        cols = [_swap_lane_blocks(slabs[4 * q:4 * q + 4]) for q in range(SSM_CH // 4)]
        pieces = [jnp.concatenate([cols[q][i] for q in range(SSM_CH // 4)], axis=1)
                  for i in range(N_SUB)]
        return jnp.concatenate(pieces, axis=0)

    def entering_states(g, s_all):
        def prow(r):
            return pw_ref[g, 0, r:r + 1, :], pw_ref[g, 1, r:r + 1, :]

        def mirror(k):
            lo, hi = k * nbc, (N_SUB - 1 - k) * nbc
            return tuple(jnp.where(is_fwd, s_all[lo:lo + nbc, c:c + LANES], s_all[hi:hi + nbc, c:c + LANES])
                         for c in (0, LANES))

        a32 = prow(0)
        sk = [mirror(k) for k in range(N_SUB)]
        zero = jnp.zeros((nbc, LANES), F32)
        pk = [(zero, zero)]
        for k in range(N_SUB - 1):
            pk.append(cadd(cmul(pk[k], a32), sk[k]))
        x = cadd(cmul(pk[N_SUB - 1], a32), sk[N_SUB - 1])
        for kk in range(_N_SCAN_STEPS):
            x = cadd(x, cmul(shifted(x, 1 << kk), prow(3 + kk)))
        ent = shifted(x, 1)
        qk = [cadd(pk[k], cmul(ent, prow(k - 1)) if k > 0 else ent) for k in range(N_SUB)]
        xin = []
        for i in range(N_SUB):
            j = N_SUB - 1 - i
            xin.append(jnp.concatenate([jnp.where(is_fwd, qk[i][c], qk[j][c]) for c in (0, 1)], axis=1))
        return jnp.concatenate(xin, axis=0).astype(BF16)

    def scatter(g, act):
        for q in range(SSM_CH // 4):
            back = _swap_lane_blocks([act[i * nbc:(i + 1) * nbc, q * LANES:(q + 1) * LANES]
                                      for i in range(N_SUB)])
            for r in range(4):
                y_ref[g * SSM_CH + 4 * q + r] = back[r]

    grp = range(n_grp)
    ust = [gather(g) for g in grp]
    ub = [u.astype(BF16) for u in ust]
    z = [_dot(ub[g], t_ref[g]) for g in grp]
    s_all = [_dot(ub[g], ms_ref[g]) for g in grp]
    xin = [entering_states(g, s_all[g]) for g in grp]
    for g in grp:
        y = z[g] + _dot(xin[g], mo_ref[g]) + dsk_ref[g] * ust[g]
        scatter(g, jax.nn.gelu(y))


def _ssm(u3, t_all, ms_all, mo_all, pw_all, dsk_all, layer, chunks_per_seq):
    nbc = u3.shape[1]
    base = layer * (SSM_G // SSM_GB)
    sel3 = lambda g: (base + g, 0, 0)
    slab_spec = pl.BlockSpec((SSM_GB * SSM_CH, nbc, LANES), lambda g: (g, 0, 0))
    return pl.pallas_call(
        functools.partial(_ssm_kernel, chunks_per_seq=chunks_per_seq),
        name="ssm_mix",
        grid=(SSM_G // SSM_GB,),
        in_specs=[
            slab_spec,
            pl.BlockSpec((SSM_GB, CW, CW), sel3),
            pl.BlockSpec((SSM_GB, CW, 2 * LANES), sel3),
            pl.BlockSpec((SSM_GB, 2 * LANES, CW), sel3),
            pl.BlockSpec((SSM_GB, 2, _POW_ROWS, LANES), lambda g: (base + g, 0, 0, 0)),
            pl.BlockSpec((SSM_GB, 1, CW), sel3),
        ],
        out_specs=slab_spec,
        out_shape=jax.ShapeDtypeStruct((SSM_W, nbc, LANES), F32),
        compiler_params=pltpu.CompilerParams(dimension_semantics=("arbitrary",),
                                             vmem_limit_bytes=VMEM_LIMIT),
    )(u3, t_all, ms_all, mo_all, pw_all, dsk_all)


def _stream_cast(pairs, layer, stage_ref, sem_ref):
    slots, rows, cols = stage_ref.shape
    tiles = [(src, dst, r, c) for src, dst in pairs
             for r in range(0, dst.shape[0], rows) for c in range(0, dst.shape[1], cols)]

    def copy(i):
        src, _, r, c = tiles[i]
        return pltpu.make_async_copy(src.at[layer, pl.ds(r, rows), pl.ds(c, cols)],
                                     stage_ref.at[i % slots], sem_ref.at[i % slots])

    for i in range(min(slots - 1, len(tiles))):
        copy(i).start()
    for i, (_, dst, r, c) in enumerate(tiles):
        if i + slots - 1 < len(tiles):
            copy(i + slots - 1).start()
        copy(i).wait()
        dst[r:r + rows, c:c + cols] = stage_ref[i % slots].astype(BF16)


def _outffn_kernel(x_ref, att_ref, y_ref, g2_ref, wglu_hbm, wout_hbm, w1_hbm, w2_hbm, o_ref,
                   wglu_ref, wout_ref, w1_ref, w2_ref, stage_ref, stage_sem, *, layer):
    @pl.when(pl.program_id(0) == 0)
    def _():
        _stream_cast([(wglu_hbm, wglu_ref), (wout_hbm, wout_ref), (w1_hbm, w1_ref), (w2_hbm, w2_ref)],
                     layer, stage_ref, stage_sem)

    first = (pl.program_id(0) % (Y_CHUNKS * LANES // TM_FFN)) * (TM_FFN // LANES)
    y2d = y_ref.reshape(SSM_W * Y_CHUNKS, LANES)
    ys = [y2d[pl.ds(first + j, SSM_W, stride=Y_CHUNKS), :].T
          for j in range(TM_FFN // LANES)]
    y = jnp.concatenate(ys, axis=0).astype(BF16)
    gl = _dot(y, wglu_ref[...])
    ssm = gl[:, :SSM_W] * jax.nn.sigmoid(gl[:, SSM_W:])
    x = x_ref[...] + _dot(att_ref[...], wout_ref[:ATT_W, :]) \
        + _dot(ssm.astype(BF16), wout_ref[ATT_W:, :])
    ms = jnp.mean(x * x, axis=-1, keepdims=True)
    hn = (x * lax.rsqrt(ms + EPS) * g2_ref[...]).astype(BF16)
    hid = jnp.maximum(_dot(hn, w1_ref[...]), 0.0)
    o_ref[...] = x + _dot((hid * hid).astype(BF16), w2_ref[...])


def _outffn(x, att, y3, g2, w_glu, w_out, w_ff1, w_ff2, layer):
    t = x.shape[0]
    hbm = pl.BlockSpec(memory_space=pl.ANY)
    return pl.pallas_call(
        functools.partial(_outffn_kernel, layer=layer),
        name="out_ffn",
        grid=(t // TM_FFN,),
        in_specs=[
            pl.BlockSpec((TM_FFN, D_MODEL), lambda i: (i, 0)),
            pl.BlockSpec((TM_FFN, ATT_W), lambda i: (i, 0)),
            pl.BlockSpec((SSM_W, Y_CHUNKS, LANES), lambda i: (0, i * TM_FFN // (Y_CHUNKS * LANES), 0)),
            pl.BlockSpec((None, 1, D_MODEL), lambda i: (layer, 0, 0)),
            hbm, hbm, hbm, hbm,
        ],
        out_specs=pl.BlockSpec((TM_FFN, D_MODEL), lambda i: (i, 0)),
        out_shape=jax.ShapeDtypeStruct((t, D_MODEL), F32),
        scratch_shapes=[
            pltpu.VMEM((SSM_W, 2 * SSM_W), BF16),
            pltpu.VMEM((D_MODEL, D_MODEL), BF16),
            pltpu.VMEM((D_MODEL, D_FF), BF16),
            pltpu.VMEM((D_FF, D_MODEL), BF16),
            pltpu.VMEM((STAGE_SLOTS, STAGE_ROWS, D_MODEL), F32),
            pltpu.SemaphoreType.DMA((STAGE_SLOTS,)),
        ],
        compiler_params=pltpu.CompilerParams(dimension_semantics=("arbitrary",),
                                             vmem_limit_bytes=VMEM_LIMIT),
    )(x, att, y3, g2, w_glu, w_out, w_ff1, w_ff2)


def _block_diag_mean(width):
    idx = jnp.arange(width) // HEAD_DIM
    return jnp.where(idx[:, None] == idx[None, :], 1.0 / HEAD_DIM, 0.0).astype(BF16)


def _prep_operands(lam_re, lam_im, log_dt, b_re, b_im, c_re, c_im, d_skip):
    depth = lam_re.shape[0]
    lg = depth * SSM_G
    both = lambda a: jnp.concatenate([a[:, 0], a[:, 1]], axis=-1)
    ldt = jnp.broadcast_to(log_dt[..., None], lam_re.shape)
    lam = jnp.stack([both(lam_re), both(lam_im), both(ldt)], axis=2).reshape(lg, 3, LANES)
    bt = jnp.stack([b_re, b_im], axis=2).transpose(0, 1, 2, 4, 3)
    bt = jnp.concatenate([bt, bt], axis=-1).reshape(lg, 2, SSM_CH, LANES)
    cc = jnp.stack([both(c_re), both(c_im)], axis=2).reshape(lg, 2, SSM_CH, LANES)
    dsk = jnp.repeat(d_skip.reshape(lg, 1, SSM_CH), SUB, axis=-1)
    return list(_ssm_prep(lam, bt, cc)) + [dsk]


def kernel(x, norm1, w_in, q_gain, k_gain, sink, lam_re, lam_im, log_dt, b_re, b_im, c_re, c_im,
           d_skip, w_glu, w_out, norm2, w_ff1, w_ff2):
    bsz, seq, _ = x.shape
    depth = norm1.shape[0]
    tokens = bsz * seq
    chunks_per_seq = seq // LANES
    assert seq % (ATT_NB * WINDOW) == 0 and tokens % TM_IN == 0 and tokens % TM_FFN == 0
    assert TM_IN == Y_CHUNKS * LANES and (Y_CHUNKS * LANES) % TM_FFN == 0
    assert chunks_per_seq == 1 << _N_SCAN_STEPS and max(_POW_N) == LANES * chunks_per_seq // 2

    t_all, ms_all, mo_all, pw_all, dsk_all = _prep_operands(lam_re, lam_im, log_dt, b_re, b_im,
                                                           c_re, c_im, d_skip)
    g1 = norm1[:, None, :]
    g2 = norm2[:, None, :]
    qg = (jnp.tile(q_gain, (1, N_HEADS)) * (LOG2E / math.sqrt(HEAD_DIM)))[:, None, :]
    kg = jnp.tile(k_gain, (1, N_KV))[:, None, :]
    bias = _attn_bias(seq)
    bdq = _block_diag_mean(ATT_W)
    bdk = _block_diag_mean(KV_W)

    xt = x.reshape(tokens, D_MODEL)
    for layer in range(depth):
        q, k, v, u3 = _inproj(xt, g1, w_in, qg, kg, bdq, bdk, layer)
        att = _attention(sink, q, k, v, bias, bsz, seq, layer)
        y3 = _ssm(u3, t_all, ms_all, mo_all, pw_all, dsk_all, layer, chunks_per_seq)
        xt = _outffn(xt, att, y3, g2, w_glu, w_out, w_ff1, w_ff2, layer)
    return xt.reshape(bsz, seq, D_MODEL)
```

```python
import functools
import math

import jax
import jax.numpy as jnp
from jax import lax
from jax.experimental import pallas as pl
from jax.experimental.pallas import tpu as pltpu

D_MODEL = 1024
N_HEADS = 8
N_KV = 2
GQA = N_HEADS // N_KV
HEAD_DIM = 64
ATT_W = N_HEADS * HEAD_DIM
KV_W = N_KV * HEAD_DIM
QKV_W = ATT_W + 2 * KV_W
WINDOW = 128
SSM_W = 512
SSM_CH = 16
SSM_G = SSM_W // SSM_CH
SSM_P = 64
D_FF = 4 * D_MODEL
EPS = 1e-6
LOG2E = math.log2(math.e)

LANES = 128
SUB = 32
N_SUB = LANES // SUB
CW = SSM_CH * SUB
VMEM_LIMIT = 56 * 1024 * 1024
STAGE_ROWS = 256
STAGE_SLOTS = 8

TM_IN = 1024
TM_HALF = 256
TM_FFN = 512
FFN_PART = 256
Y_CHUNKS = 8
ATT_NB = 4
SSM_GB = 2
PREP_GB = 4

_POW_N = (32, 64, 96, 128, 256, 512, 1024, 2048, 4096)
_POW_ROWS = 16
_N_SCAN_STEPS = 6

F32 = jnp.float32
BF16 = jnp.bfloat16
I32 = jnp.int32


def _dot(a, b):
    return jnp.dot(a, b, preferred_element_type=F32)


def _dot_nt(a, b, precision=None):
    return lax.dot_general(a, b, (((1,), (1,)), ((), ())), precision=precision,
                           preferred_element_type=F32)


def _prep_kernel(lam_ref, bt_ref, c_ref, t_ref, ms_ref, mo_ref, pw_ref,
                 g1r_ref, g1i_ref, mtr_ref, mti_ref):
    grp = range(lam_ref.shape[0])
    s_idx = lax.broadcasted_iota(I32, (SUB, LANES), 0)
    is_fwd = lax.broadcasted_iota(I32, (SUB, LANES), 1) < SSM_P
    fwd16 = lax.broadcasted_iota(I32, (SSM_CH, LANES), 1) < SSM_P
    e_t = jnp.where(is_fwd, s_idx, SUB - 1 - s_idx)

    def cmul(x, y):
        return x[0] * y[0] - x[1] * y[1], x[0] * y[1] + x[1] * y[0]

    def discretise(g):
        lr = lam_ref[g, 0:1, :]
        li = lam_ref[g, 1:2, :]
        dt = jnp.exp(lam_ref[g, 2:3, :])
        mag = jnp.exp(lr * dt)
        abr = mag * jnp.cos(li * dt)
        abi = mag * jnp.sin(li * dt)
        den = lr * lr + li * li
        zr = ((abr - 1.0) * lr + abi * li) / den
        zi = (abi * lr - (abr - 1.0) * li) / den
        btr = bt_ref[g, 0]
        bti = bt_ref[g, 1]
        sq = [(abr, abi)]
        for _ in range(max(_POW_N).bit_length() - 1):
            sq.append(cmul(sq[-1], sq[-1]))
        return sq, zr * btr - zi * bti, zr * bti + zi * btr

    def int_power(sq, e):
        acc = None
        for bit in range(SUB.bit_length() - 1):
            on = (e & (1 << bit)) != 0
            fac = (jnp.where(on, sq[bit][0], 1.0), jnp.where(on, sq[bit][1], 0.0))
            acc = fac if acc is None else cmul(acc, fac)
        return acc

    disc = [discretise(g) for g in grp]
    tpow = [int_power(disc[g][0], e_t) for g in grp]
    mspow = [int_power(disc[g][0], SUB - 1 - e_t) for g in grp]
    mopow = [cmul(tpow[g], disc[g][0][0]) for g in grp]

    for g in grp:
        _, bbr, bbi = disc[g]
        cr = c_ref[g, 0]
        ci = c_ref[g, 1]
        (tr, ti), (msr, msi), (mor, moi) = tpow[g], mspow[g], mopow[g]
        for h in range(SSM_CH):
            rows = slice(h * SUB, (h + 1) * SUB)
            b_r = bbr[h:h + 1, :]
            b_i = bbi[h:h + 1, :]
            ms_ref[g, rows, :LANES] = (msr * b_r - msi * b_i).astype(BF16)
            ms_ref[g, rows, LANES:] = (msr * b_i + msi * b_r).astype(BF16)
            c_r = cr[h:h + 1, :]
            c_i = ci[h:h + 1, :]
            g1r_ref[g, rows, :] = tr * c_r - ti * c_i
            g1i_ref[g, rows, :] = tr * c_i + ti * c_r
            mtr_ref[g, rows, :] = mor * c_r - moi * c_i
            mti_ref[g, rows, :] = -(mor * c_i + moi * c_r)
    for g in grp:
        mo_ref[g, :LANES, :] = mtr_ref[g].T.astype(BF16)
        mo_ref[g, LANES:, :] = mti_ref[g].T.astype(BF16)

    hp = lax.Precision.HIGHEST
    kts = []
    for g in grp:
        _, bbr, bbi = disc[g]
        g1r = g1r_ref[g]
        g1i = g1i_ref[g]
        kts.append([_dot_nt(jnp.where(sel, bbr, 0.0), g1r, precision=hp)
                    - _dot_nt(jnp.where(sel, bbi, 0.0), g1i, precision=hp)
                    for sel in (fwd16, jnp.logical_not(fwd16))])

    for g in grp:
        sq = disc[g][0]
        pw_ref[g] = jnp.zeros((2, _POW_ROWS, LANES), F32)
        for r, n in enumerate(_POW_N):
            acc = None
            for bit in range(n.bit_length()):
                if n >> bit & 1:
                    acc = sq[bit] if acc is None else cmul(acc, sq[bit])
            pw_ref[g, 0, r:r + 1, :] = acc[0]
            pw_ref[g, 1, r:r + 1, :] = acc[1]

    s_lane = lax.broadcasted_iota(I32, (SUB, CW), 1) & (SUB - 1)
    s_row = lax.broadcasted_iota(I32, (SUB, CW), 0)
    fwd_mask = s_lane >= s_row
    tau0 = (lax.broadcasted_iota(I32, (SSM_CH, CW), 1) & (SUB - 1)) == 0
    for g in grp:
        kfd = kts[g][0] + jnp.where(tau0, pltpu.roll(kts[g][1], CW - (SUB - 1), 1), 0.0)
        for h in range(SSM_CH):
            kf = jnp.broadcast_to(kfd[h:h + 1, :], (SUB, CW))
            kb = jnp.broadcast_to(kts[g][1][h:h + 1, :], (SUB, CW))
            rf = pltpu.roll(kf, 0, 1, stride=1, stride_axis=0)
            rb = pltpu.roll(kb, CW - (SUB - 1), 1, stride=1, stride_axis=0)
            t_ref[g, h * SUB:(h + 1) * SUB, :] = jnp.where(fwd_mask, rf, rb).astype(BF16)


def _ssm_prep(lam, bt, c):
    lg = lam.shape[0]
    sel3 = lambda i: (i, 0, 0)
    sel4 = lambda i: (i, 0, 0, 0)
    return pl.pallas_call(
        _prep_kernel,
        name="ssm_prep",
        grid=(lg // PREP_GB,),
        in_specs=[
            pl.BlockSpec((PREP_GB, 3, LANES), sel3),
            pl.BlockSpec((PREP_GB, 2, SSM_CH, LANES), sel4),
            pl.BlockSpec((PREP_GB, 2, SSM_CH, LANES), sel4),
        ],
        out_specs=[
            pl.BlockSpec((PREP_GB, CW, CW), sel3),
            pl.BlockSpec((PREP_GB, CW, 2 * LANES), sel3),
            pl.BlockSpec((PREP_GB, 2 * LANES, CW), sel3),
            pl.BlockSpec((PREP_GB, 2, _POW_ROWS, LANES), sel4),
        ],
        out_shape=[
            jax.ShapeDtypeStruct((lg, CW, CW), BF16),
            jax.ShapeDtypeStruct((lg, CW, 2 * LANES), BF16),
            jax.ShapeDtypeStruct((lg, 2 * LANES, CW), BF16),
            jax.ShapeDtypeStruct((lg, 2, _POW_ROWS, LANES), F32),
        ],
        scratch_shapes=[pltpu.VMEM((PREP_GB, CW, LANES), F32)] * 4,
        compiler_params=pltpu.CompilerParams(dimension_semantics=("arbitrary",)),
    )(lam, bt, c)


def _inproj_kernel(x_ref, g1_ref, win_ref, qg_ref, kg_ref, bdq_ref, bdk_ref,
                   q_ref, k_ref, v_ref, u_ref, wqkv_ref, wut_ref):
    @pl.when(pl.program_id(0) == 0)
    def _():
        wqkv_ref[...] = win_ref[:, :QKV_W].astype(BF16)
        wut_ref[...] = win_ref[:, QKV_W:].T.astype(BF16)

    def head_norm(t, bd_ref, gain):
        m = _dot((t * t).astype(BF16), bd_ref[...])
        return t * lax.rsqrt(m + EPS) * gain

    halves = [slice(r * TM_HALF, (r + 1) * TM_HALF) for r in range(TM_IN // TM_HALF)]
    hn = []
    for rows in halves:
        x = x_ref[rows, :]
        ms = jnp.mean(x * x, axis=-1, keepdims=True)
        hn.append((x * lax.rsqrt(ms + EPS) * g1_ref[...]).astype(BF16))
    z = [_dot(h, wqkv_ref[...]) for h in hn]
    ut = [_dot_nt(wut_ref[...], h) for h in hn]
    for rows, zz in zip(halves, z):
        q_ref[rows, :] = head_norm(zz[:, :ATT_W], bdq_ref, qg_ref[...]).astype(BF16)
        k_ref[rows, :] = head_norm(zz[:, ATT_W:ATT_W + KV_W], bdk_ref, kg_ref[...]).astype(BF16)
        v_ref[rows, :] = zz[:, ATT_W + KV_W:].astype(BF16)
    chunks = TM_IN // LANES
    u2d = u_ref.reshape(SSM_W * chunks, LANES)
    for r, utr in enumerate(ut):
        for jj in range(TM_HALF // LANES):
            j = r * (TM_HALF // LANES) + jj
            u2d[pl.ds(j, SSM_W, stride=chunks), :] = utr[:, jj * LANES:(jj + 1) * LANES]


def _inproj(x, g1, w_in, qg, kg, bdq, bdk, layer):
    t = x.shape[0]
    const = lambda i: (0, 0)
    sel = lambda i: (layer, 0, 0)
    return pl.pallas_call(
        _inproj_kernel,
        name="inproj",
        grid=(t // TM_IN,),
        in_specs=[
            pl.BlockSpec((TM_IN, D_MODEL), lambda i: (i, 0)),
            pl.BlockSpec((None, 1, D_MODEL), sel),
            pl.BlockSpec((None, D_MODEL, QKV_W + SSM_W), sel, pipeline_mode=pl.Buffered(1)),
            pl.BlockSpec((None, 1, ATT_W), sel),
            pl.BlockSpec((None, 1, KV_W), sel),
            pl.BlockSpec((ATT_W, ATT_W), const),
            pl.BlockSpec((KV_W, KV_W), const),
        ],
        out_specs=[
            pl.BlockSpec((TM_IN, ATT_W), lambda i: (i, 0)),
            pl.BlockSpec((TM_IN, KV_W), lambda i: (i, 0)),
            pl.BlockSpec((TM_IN, KV_W), lambda i: (i, 0)),
            pl.BlockSpec((SSM_W, TM_IN // LANES, LANES), lambda i: (0, i, 0)),
        ],
        out_shape=[
            jax.ShapeDtypeStruct((t, ATT_W), BF16),
            jax.ShapeDtypeStruct((t, KV_W), BF16),
            jax.ShapeDtypeStruct((t, KV_W), BF16),
            jax.ShapeDtypeStruct((SSM_W, t // LANES, LANES), F32),
        ],
        scratch_shapes=[pltpu.VMEM((D_MODEL, QKV_W), BF16), pltpu.VMEM((SSM_W, D_MODEL), BF16)],
        compiler_params=pltpu.CompilerParams(dimension_semantics=("arbitrary",),
                                             vmem_limit_bytes=VMEM_LIMIT),
    )(x, g1, w_in, qg, kg, bdq, bdk)


def _attn_kernel(sink_ref, q_ref, k_ref, v_ref, bias_ref, o_ref, s_scr, p_scr, *, n_blocks):
    n0 = pl.program_id(1) * ATT_NB
    prev = pl.multiple_of(jnp.maximum(n0 - 1, 0) * WINDOW, WINDOW)
    mid = pl.multiple_of(n0 * WINDOW, WINDOW)
    nxt = pl.multiple_of(jnp.minimum(n0 + ATT_NB, n_blocks - 1) * WINDOW, WINDOW)

    def window(ref):
        return jnp.concatenate([ref[0, pl.ds(prev, WINDOW), :], ref[0, pl.ds(mid, ATT_NB * WINDOW), :],
                                ref[0, pl.ds(nxt, WINDOW), :]], axis=0)

    kw = window(k_ref)
    vw = window(v_ref)
    kj = [kw[:, j * HEAD_DIM:(j + 1) * HEAD_DIM] for j in range(N_KV)]
    vj = [vw[:, j * HEAD_DIM:(j + 1) * HEAD_DIM] for j in range(N_KV)]

    def scores(b):
        n = n0 + b
        var = jnp.where(n == 0, 0, jnp.where(n == n_blocks - 1, 2, 1))
        qb = q_ref[b * WINDOW:(b + 1) * WINDOW, :]
        for h in range(N_HEADS):
            kh = kj[h // GQA][b * WINDOW:(b + 3) * WINDOW, :]
            s_scr[b % 2, h] = _dot_nt(qb[:, h * HEAD_DIM:(h + 1) * HEAD_DIM], kh) + bias_ref[var, h]

    def softmax(b):
        inv = []
        for h in range(N_HEADS):
            s = s_scr[b % 2, h]
            sk = sink_ref[h] * LOG2E
            m = jnp.maximum(jnp.max(s, axis=-1, keepdims=True), sk)
            p = jnp.exp2(s - m)
            inv.append(1.0 / (jnp.sum(p, axis=-1, keepdims=True) + jnp.exp2(sk - m)))
            p_scr[b % 2, h] = p.astype(BF16)
        return inv

    def weighted(b, inv):
        outs = []
        for h in range(N_HEADS):
            vh = vj[h // GQA][b * WINDOW:(b + 3) * WINDOW, :]
            outs.append(_dot(p_scr[b % 2, h], vh) * inv[h])
        o_ref[b * WINDOW:(b + 1) * WINDOW, :] = jnp.concatenate(outs, axis=-1).astype(BF16)

    scores(0)
    for b in range(ATT_NB):
        if b + 1 < ATT_NB:
            scores(b + 1)
        weighted(b, softmax(b))


def _attention(sink, q, k, v, bias, bsz, seq, layer):
    n_blocks = seq // WINDOW
    steps = n_blocks // ATT_NB
    rows = ATT_NB * WINDOW
    return pl.pallas_call(
        functools.partial(_attn_kernel, n_blocks=n_blocks),
        name="band_attn",
        grid=(bsz, steps),
        in_specs=[
            pl.BlockSpec(memory_space=pltpu.SMEM),
            pl.BlockSpec((rows, ATT_W), lambda b, n: (b * steps + n, 0)),
            pl.BlockSpec((1, seq, KV_W), lambda b, n: (b, 0, 0)),
            pl.BlockSpec((1, seq, KV_W), lambda b, n: (b, 0, 0)),
            pl.BlockSpec((3, N_HEADS, WINDOW, 3 * WINDOW), lambda b, n: (0, 0, 0, 0)),
        ],
        out_specs=pl.BlockSpec((rows, ATT_W), lambda b, n: (b * steps + n, 0)),
        out_shape=jax.ShapeDtypeStruct((bsz * seq, ATT_W), BF16),
        scratch_shapes=[pltpu.VMEM((2, N_HEADS, WINDOW, 3 * WINDOW), F32),
                        pltpu.VMEM((2, N_HEADS, WINDOW, 3 * WINDOW), BF16)],
        compiler_params=pltpu.CompilerParams(dimension_semantics=("arbitrary", "arbitrary"),
                                             vmem_limit_bytes=VMEM_LIMIT),
    )(sink[layer], q, k.reshape(bsz, seq, KV_W), v.reshape(bsz, seq, KV_W), bias)


def _attn_bias(seq):
    q_idx = jnp.arange(WINDOW)[:, None]
    c_idx = jnp.arange(3 * WINDOW)[None, :]
    dist = jnp.abs(q_idx - c_idx + WINDOW)
    slopes = jnp.exp2(-8.0 * jnp.arange(1, N_HEADS + 1, dtype=F32) / N_HEADS)
    alibi = -slopes[:, None, None] * dist.astype(F32)[None]
    in_band = (dist <= WINDOW)[None]
    blk = (c_idx // WINDOW)[None]
    variants = []
    for bad in (0, -1, 2):
        ok = in_band & (blk != bad)
        variants.append(jnp.where(ok, alibi * LOG2E, -1e30))
    return jnp.stack(variants).astype(F32)


def _swap_lane_blocks(vs):
    lane = lax.broadcasted_iota(I32, vs[0].shape, 1)
    lo_half = lane < 2 * SUB
    even_blk = (lane & SUB) == 0
    a0, a1, a2, a3 = vs
    c0 = jnp.where(lo_half, a0, pltpu.roll(a2, 2 * SUB, 1))
    c2 = jnp.where(lo_half, pltpu.roll(a0, 2 * SUB, 1), a2)
    c1 = jnp.where(lo_half, a1, pltpu.roll(a3, 2 * SUB, 1))
    c3 = jnp.where(lo_half, pltpu.roll(a1, 2 * SUB, 1), a3)
    b0 = jnp.where(even_blk, c0, pltpu.roll(c1, SUB, 1))
    b1 = jnp.where(even_blk, pltpu.roll(c0, LANES - SUB, 1), c1)
    b2 = jnp.where(even_blk, c2, pltpu.roll(c3, SUB, 1))
    b3 = jnp.where(even_blk, pltpu.roll(c2, LANES - SUB, 1), c3)
    return [b0, b1, b2, b3]


def _ssm_kernel(u_ref, t_ref, ms_ref, mo_ref, pw_ref, dsk_ref, y_ref, *, chunks_per_seq):
    n_grp = u_ref.shape[0] // SSM_CH
    nbc = u_ref.shape[1]
    is_fwd = lax.broadcasted_iota(I32, (nbc, LANES), 1) < SSM_P
    c_idx = lax.broadcasted_iota(I32, (nbc, LANES), 0) & (chunks_per_seq - 1)

    def cmul(x, a):
        return a[0] * x[0] - a[1] * x[1], a[0] * x[1] + a[1] * x[0]

    def cadd(x, y):
        return x[0] + y[0], x[1] + y[1]

    def shifted(x, dd):
        ok_f = jnp.logical_and(is_fwd, c_idx >= dd)
        ok_b = jnp.logical_and(jnp.logical_not(is_fwd), c_idx < chunks_per_seq - dd)
        return tuple(jnp.where(ok_f, pltpu.roll(t, dd, 0), jnp.where(ok_b, pltpu.roll(t, nbc - dd, 0), 0.0))
                     for t in x)

    def gather(g):
        slabs = [u_ref[g * SSM_CH + h] for h in range(SSM_CH)]
        cols = [_swap_lane_blocks(slabs[4 * q:4 * q + 4]) for q in range(SSM_CH // 4)]
        pieces = [jnp.concatenate([cols[q][i] for q in range(SSM_CH // 4)], axis=1)
                  for i in range(N_SUB)]
        return jnp.concatenate(pieces, axis=0)

    def entering_states(g, s_all):
        def prow(r):
            return pw_ref[g, 0, r:r + 1, :], pw_ref[g, 1, r:r + 1, :]

        def mirror(k):
            lo, hi = k * nbc, (N_SUB - 1 - k) * nbc
            return tuple(jnp.where(is_fwd, s_all[lo:lo + nbc, c:c + LANES], s_all[hi:hi + nbc, c:c + LANES])
                         for c in (0, LANES))

        a32 = prow(0)
        sk = [mirror(k) for k in range(N_SUB)]
        zero = jnp.zeros((nbc, LANES), F32)
        pk = [(zero, zero)]
        for k in range(N_SUB - 1):
            pk.append(cadd(cmul(pk[k], a32), sk[k]))
        x = cadd(cmul(pk[N_SUB - 1], a32), sk[N_SUB - 1])
        for kk in range(_N_SCAN_STEPS):
            x = cadd(x, cmul(shifted(x, 1 << kk), prow(3 + kk)))
        ent = shifted(x, 1)
        qk = [cadd(pk[k], cmul(ent, prow(k - 1)) if k > 0 else ent) for k in range(N_SUB)]
        xin = []
        for i in range(N_SUB):
            j = N_SUB - 1 - i
            xin.append(jnp.concatenate([jnp.where(is_fwd, qk[i][c], qk[j][c]) for c in (0, 1)], axis=1))
        return jnp.concatenate(xin, axis=0).astype(BF16)

    def scatter(g, act):
        for q in range(SSM_CH // 4):
            back = _swap_lane_blocks([act[i * nbc:(i + 1) * nbc, q * LANES:(q + 1) * LANES]
                                      for i in range(N_SUB)])
            for r in range(4):
                y_ref[g * SSM_CH + 4 * q + r] = back[r]

    grp = range(n_grp)
    ust = [gather(g) for g in grp]
    ub = [u.astype(BF16) for u in ust]
    z = [_dot(ub[g], t_ref[g]) for g in grp]
    s_all = [_dot(ub[g], ms_ref[g]) for g in grp]
    xin = [entering_states(g, s_all[g]) for g in grp]
    for g in grp:
        y = z[g] + _dot(xin[g], mo_ref[g]) + dsk_ref[g] * ust[g]
        scatter(g, jax.nn.gelu(y))


def _ssm(u3, t_all, ms_all, mo_all, pw_all, dsk_all, layer, chunks_per_seq):
    nbc = u3.shape[1]
    base = layer * (SSM_G // SSM_GB)
    sel3 = lambda g: (base + g, 0, 0)
    slab_spec = pl.BlockSpec((SSM_GB * SSM_CH, nbc, LANES), lambda g: (g, 0, 0))
    return pl.pallas_call(
        functools.partial(_ssm_kernel, chunks_per_seq=chunks_per_seq),
        name="ssm_mix",
        grid=(SSM_G // SSM_GB,),
        in_specs=[
            slab_spec,
            pl.BlockSpec((SSM_GB, CW, CW), sel3),
            pl.BlockSpec((SSM_GB, CW, 2 * LANES), sel3),
            pl.BlockSpec((SSM_GB, 2 * LANES, CW), sel3),
            pl.BlockSpec((SSM_GB, 2, _POW_ROWS, LANES), lambda g: (base + g, 0, 0, 0)),
            pl.BlockSpec((SSM_GB, 1, CW), sel3),
        ],
        out_specs=slab_spec,
        out_shape=jax.ShapeDtypeStruct((SSM_W, nbc, LANES), F32),
        compiler_params=pltpu.CompilerParams(dimension_semantics=("arbitrary",),
                                             vmem_limit_bytes=VMEM_LIMIT),
    )(u3, t_all, ms_all, mo_all, pw_all, dsk_all)


def _stream_cast(pairs, layer, stage_ref, sem_ref):
    slots, rows, cols = stage_ref.shape
    tiles = [(src, dst, r, c) for src, dst in pairs
             for r in range(0, dst.shape[0], rows) for c in range(0, dst.shape[1], cols)]

    def copy(i):
        src, _, r, c = tiles[i]
        return pltpu.make_async_copy(src.at[layer, pl.ds(r, rows), pl.ds(c, cols)],
                                     stage_ref.at[i % slots], sem_ref.at[i % slots])

    for i in range(min(slots - 1, len(tiles))):
        copy(i).start()
    for i, (_, dst, r, c) in enumerate(tiles):
        if i + slots - 1 < len(tiles):
            copy(i + slots - 1).start()
        copy(i).wait()
        dst[r:r + rows, c:c + cols] = stage_ref[i % slots].astype(BF16)


def _outffn_kernel(x_ref, att_ref, y_ref, g2_ref, wglu_hbm, wout_hbm, w1_hbm, w2_hbm, o_ref,
                   wglu_ref, wout_ref, w1_ref, w2_ref, stage_ref, stage_sem, *, layer):
    @pl.when(pl.program_id(0) == 0)
    def _():
        _stream_cast([(wglu_hbm, wglu_ref), (wout_hbm, wout_ref), (w1_hbm, w1_ref), (w2_hbm, w2_ref)],
                     layer, stage_ref, stage_sem)

    first = (pl.program_id(0) % (Y_CHUNKS * LANES // TM_FFN)) * (TM_FFN // LANES)
    y2d = y_ref.reshape(SSM_W * Y_CHUNKS, LANES)
    parts = [slice(r * FFN_PART, (r + 1) * FFN_PART) for r in range(TM_FFN // FFN_PART)]
    per = FFN_PART // LANES
    y = [jnp.concatenate([y2d[pl.ds(first + r * per + j, SSM_W, stride=Y_CHUNKS), :].T
                          for j in range(per)], axis=0).astype(BF16)
         for r in range(len(parts))]
    gl = [_dot(yy, wglu_ref[...]) for yy in y]
    ssm = [(g[:, :SSM_W] * jax.nn.sigmoid(g[:, SSM_W:])).astype(BF16) for g in gl]
    x = [x_ref[rows, :] + _dot(att_ref[rows, :], wout_ref[:ATT_W, :]) + _dot(s, wout_ref[ATT_W:, :])
         for rows, s in zip(parts, ssm)]
    hn = []
    for xx in x:
        ms = jnp.mean(xx * xx, axis=-1, keepdims=True)
        hn.append((xx * lax.rsqrt(ms + EPS) * g2_ref[...]).astype(BF16))
    hid = [jnp.maximum(_dot(h, w1_ref[...]), 0.0) for h in hn]
    for rows, xx, hh in zip(parts, x, hid):
        o_ref[rows, :] = xx + _dot((hh * hh).astype(BF16), w2_ref[...])


def _outffn(x, att, y3, g2, w_glu, w_out, w_ff1, w_ff2, layer):
    t = x.shape[0]
    hbm = pl.BlockSpec(memory_space=pl.ANY)
    return pl.pallas_call(
        functools.partial(_outffn_kernel, layer=layer),
        name="out_ffn",
        grid=(t // TM_FFN,),
        in_specs=[
            pl.BlockSpec((TM_FFN, D_MODEL), lambda i: (i, 0)),
            pl.BlockSpec((TM_FFN, ATT_W), lambda i: (i, 0)),
            pl.BlockSpec((SSM_W, Y_CHUNKS, LANES), lambda i: (0, i * TM_FFN // (Y_CHUNKS * LANES), 0)),
            pl.BlockSpec((None, 1, D_MODEL), lambda i: (layer, 0, 0)),
            hbm, hbm, hbm, hbm,
        ],
        out_specs=pl.BlockSpec((TM_FFN, D_MODEL), lambda i: (i, 0)),
        out_shape=jax.ShapeDtypeStruct((t, D_MODEL), F32),
        scratch_shapes=[
            pltpu.VMEM((SSM_W, 2 * SSM_W), BF16),
            pltpu.VMEM((D_MODEL, D_MODEL), BF16),
            pltpu.VMEM((D_MODEL, D_FF), BF16),
            pltpu.VMEM((D_FF, D_MODEL), BF16),
            pltpu.VMEM((STAGE_SLOTS, STAGE_ROWS, D_MODEL), F32),
            pltpu.SemaphoreType.DMA((STAGE_SLOTS,)),
        ],
        compiler_params=pltpu.CompilerParams(dimension_semantics=("arbitrary",),
                                             vmem_limit_bytes=VMEM_LIMIT),
    )(x, att, y3, g2, w_glu, w_out, w_ff1, w_ff2)


def _block_diag_mean(width):
    idx = jnp.arange(width) // HEAD_DIM
    return jnp.where(idx[:, None] == idx[None, :], 1.0 / HEAD_DIM, 0.0).astype(BF16)


def _prep_operands(lam_re, lam_im, log_dt, b_re, b_im, c_re, c_im, d_skip):
    depth = lam_re.shape[0]
    lg = depth * SSM_G
    both = lambda a: jnp.concatenate([a[:, 0], a[:, 1]], axis=-1)
    ldt = jnp.broadcast_to(log_dt[..., None], lam_re.shape)
    lam = jnp.stack([both(lam_re), both(lam_im), both(ldt)], axis=2).reshape(lg, 3, LANES)
    bt = jnp.stack([b_re, b_im], axis=2).transpose(0, 1, 2, 4, 3)
    bt = jnp.concatenate([bt, bt], axis=-1).reshape(lg, 2, SSM_CH, LANES)
    cc = jnp.stack([both(c_re), both(c_im)], axis=2).reshape(lg, 2, SSM_CH, LANES)
    dsk = jnp.repeat(d_skip.reshape(lg, 1, SSM_CH), SUB, axis=-1)
    return list(_ssm_prep(lam, bt, cc)) + [dsk]


def kernel(x, norm1, w_in, q_gain, k_gain, sink, lam_re, lam_im, log_dt, b_re, b_im, c_re, c_im,
           d_skip, w_glu, w_out, norm2, w_ff1, w_ff2):
    bsz, seq, _ = x.shape
    depth = norm1.shape[0]
    tokens = bsz * seq
    chunks_per_seq = seq // LANES
    assert seq % (ATT_NB * WINDOW) == 0 and tokens % TM_IN == 0 and tokens % TM_FFN == 0
    assert TM_IN == Y_CHUNKS * LANES and (Y_CHUNKS * LANES) % TM_FFN == 0
    assert chunks_per_seq == 1 << _N_SCAN_STEPS and max(_POW_N) == LANES * chunks_per_seq // 2

    t_all, ms_all, mo_all, pw_all, dsk_all = _prep_operands(lam_re, lam_im, log_dt, b_re, b_im,
                                                           c_re, c_im, d_skip)
    g1 = norm1[:, None, :]
    g2 = norm2[:, None, :]
    qg = (jnp.tile(q_gain, (1, N_HEADS)) * (LOG2E / math.sqrt(HEAD_DIM)))[:, None, :]
    kg = jnp.tile(k_gain, (1, N_KV))[:, None, :]
    bias = _attn_bias(seq)
    bdq = _block_diag_mean(ATT_W)
    bdk = _block_diag_mean(KV_W)

    xt = x.reshape(tokens, D_MODEL)
    for layer in range(depth):
        q, k, v, u3 = _inproj(xt, g1, w_in, qg, kg, bdq, bdk, layer)
        att = _attention(sink, q, k, v, bias, bsz, seq, layer)
        y3 = _ssm(u3, t_all, ms_all, mo_all, pw_all, dsk_all, layer, chunks_per_seq)
        xt = _outffn(xt, att, y3, g2, w_glu, w_out, w_ff1, w_ff2, layer)
    return xt.reshape(bsz, seq, D_MODEL)
```

```python
import functools
import math

import jax
import jax.numpy as jnp
from jax import lax
from jax.experimental import pallas as pl
from jax.experimental.pallas import tpu as pltpu

D_MODEL = 1024
N_HEADS = 8
N_KV = 2
GQA = N_HEADS // N_KV
HEAD_DIM = 64
ATT_W = N_HEADS * HEAD_DIM
KV_W = N_KV * HEAD_DIM
QKV_W = ATT_W + 2 * KV_W
WINDOW = 128
SSM_W = 512
SSM_CH = 16
SSM_G = SSM_W // SSM_CH
SSM_P = 64
D_FF = 4 * D_MODEL
EPS = 1e-6
LOG2E = math.log2(math.e)

LANES = 128
SUB = 32
N_SUB = LANES // SUB
CW = SSM_CH * SUB
VMEM_LIMIT = 56 * 1024 * 1024
STAGE_ROWS = 256
STAGE_SLOTS = 8

TM_IN = 1024
TM_HALF = 256
TM_FFN = 512
FFN_PART = 256
Y_CHUNKS = 8
ATT_NB = 4
SSM_GB = 2
PREP_GB = 4

_POW_N = (32, 64, 96, 128, 256, 512, 1024, 2048, 4096)
_POW_ROWS = 16
_N_SCAN_STEPS = 6

F32 = jnp.float32
BF16 = jnp.bfloat16
I32 = jnp.int32


def _dot(a, b):
    return jnp.dot(a, b, preferred_element_type=F32)


def _dot_nt(a, b, precision=None):
    return lax.dot_general(a, b, (((1,), (1,)), ((), ())), precision=precision,
                           preferred_element_type=F32)


def _prep_kernel(lam_ref, bt_ref, c_ref, t_ref, ms_ref, mo_ref, pw_ref,
                 g1r_ref, g1i_ref, mtr_ref, mti_ref):
    grp = range(lam_ref.shape[0])
    s_idx = lax.broadcasted_iota(I32, (SUB, LANES), 0)
    is_fwd = lax.broadcasted_iota(I32, (SUB, LANES), 1) < SSM_P
    fwd16 = lax.broadcasted_iota(I32, (SSM_CH, LANES), 1) < SSM_P
    e_t = jnp.where(is_fwd, s_idx, SUB - 1 - s_idx)

    def cmul(x, y):
        return x[0] * y[0] - x[1] * y[1], x[0] * y[1] + x[1] * y[0]

    def discretise(g):
        lr = lam_ref[g, 0:1, :]
        li = lam_ref[g, 1:2, :]
        dt = jnp.exp(lam_ref[g, 2:3, :])
        mag = jnp.exp(lr * dt)
        abr = mag * jnp.cos(li * dt)
        abi = mag * jnp.sin(li * dt)
        den = lr * lr + li * li
        zr = ((abr - 1.0) * lr + abi * li) / den
        zi = (abi * lr - (abr - 1.0) * li) / den
        btr = bt_ref[g, 0]
        bti = bt_ref[g, 1]
        sq = [(abr, abi)]
        for _ in range(max(_POW_N).bit_length() - 1):
            sq.append(cmul(sq[-1], sq[-1]))
        return sq, zr * btr - zi * bti, zr * bti + zi * btr

    def int_power(sq, e):
        acc = None
        for bit in range(SUB.bit_length() - 1):
            on = (e & (1 << bit)) != 0
            fac = (jnp.where(on, sq[bit][0], 1.0), jnp.where(on, sq[bit][1], 0.0))
            acc = fac if acc is None else cmul(acc, fac)
        return acc

    disc = [discretise(g) for g in grp]
    tpow = [int_power(disc[g][0], e_t) for g in grp]
    mspow = [int_power(disc[g][0], SUB - 1 - e_t) for g in grp]
    mopow = [cmul(tpow[g], disc[g][0][0]) for g in grp]

    for g in grp:
        _, bbr, bbi = disc[g]
        cr = c_ref[g, 0]
        ci = c_ref[g, 1]
        (tr, ti), (msr, msi), (mor, moi) = tpow[g], mspow[g], mopow[g]
        for h in range(SSM_CH):
            rows = slice(h * SUB, (h + 1) * SUB)
            b_r = bbr[h:h + 1, :]
            b_i = bbi[h:h + 1, :]
            ms_ref[g, rows, :LANES] = (msr * b_r - msi * b_i).astype(BF16)
            ms_ref[g, rows, LANES:] = (msr * b_i + msi * b_r).astype(BF16)
            c_r = cr[h:h + 1, :]
            c_i = ci[h:h + 1, :]
            g1r_ref[g, rows, :] = tr * c_r - ti * c_i
            g1i_ref[g, rows, :] = tr * c_i + ti * c_r
            mtr_ref[g, rows, :] = mor * c_r - moi * c_i
            mti_ref[g, rows, :] = -(mor * c_i + moi * c_r)
    for g in grp:
        mo_ref[g, :LANES, :] = mtr_ref[g].T.astype(BF16)
        mo_ref[g, LANES:, :] = mti_ref[g].T.astype(BF16)

    def split(t):
        hi = t.astype(BF16)
        return hi, (t - hi.astype(F32)).astype(BF16)

    def dot_nt_x3(a, b):
        return _dot_nt(a[0], b[0]) + (_dot_nt(a[0], b[1]) + _dot_nt(a[1], b[0]))

    kts = []
    for g in grp:
        _, bbr, bbi = disc[g]
        g1r = split(g1r_ref[g])
        g1i = split(g1i_ref[g])
        kts.append([dot_nt_x3(split(jnp.where(sel, bbr, 0.0)), g1r)
                    - dot_nt_x3(split(jnp.where(sel, bbi, 0.0)), g1i)
                    for sel in (fwd16, jnp.logical_not(fwd16))])

    for g in grp:
        sq = disc[g][0]
        pw_ref[g] = jnp.zeros((2, _POW_ROWS, LANES), F32)
        for r, n in enumerate(_POW_N):
            acc = None
            for bit in range(n.bit_length()):
                if n >> bit & 1:
                    acc = sq[bit] if acc is None else cmul(acc, sq[bit])
            pw_ref[g, 0, r:r + 1, :] = acc[0]
            pw_ref[g, 1, r:r + 1, :] = acc[1]

    s_lane = lax.broadcasted_iota(I32, (SUB, CW), 1) & (SUB - 1)
    s_row = lax.broadcasted_iota(I32, (SUB, CW), 0)
    fwd_mask = s_lane >= s_row
    tau0 = (lax.broadcasted_iota(I32, (SSM_CH, CW), 1) & (SUB - 1)) == 0
    for g in grp:
        kfd = kts[g][0] + jnp.where(tau0, pltpu.roll(kts[g][1], CW - (SUB - 1), 1), 0.0)
        for h in range(SSM_CH):
            kf = jnp.broadcast_to(kfd[h:h + 1, :], (SUB, CW))
            kb = jnp.broadcast_to(kts[g][1][h:h + 1, :], (SUB, CW))
            rf = pltpu.roll(kf, 0, 1, stride=1, stride_axis=0)
            rb = pltpu.roll(kb, CW - (SUB - 1), 1, stride=1, stride_axis=0)
            t_ref[g, h * SUB:(h + 1) * SUB, :] = jnp.where(fwd_mask, rf, rb).astype(BF16)


def _ssm_prep(lam, bt, c):
    lg = lam.shape[0]
    sel3 = lambda i: (i, 0, 0)
    sel4 = lambda i: (i, 0, 0, 0)
    return pl.pallas_call(
        _prep_kernel,
        name="ssm_prep",
        grid=(lg // PREP_GB,),
        in_specs=[
            pl.BlockSpec((PREP_GB, 3, LANES), sel3),
            pl.BlockSpec((PREP_GB, 2, SSM_CH, LANES), sel4),
            pl.BlockSpec((PREP_GB, 2, SSM_CH, LANES), sel4),
        ],
        out_specs=[
            pl.BlockSpec((PREP_GB, CW, CW), sel3),
            pl.BlockSpec((PREP_GB, CW, 2 * LANES), sel3),
            pl.BlockSpec((PREP_GB, 2 * LANES, CW), sel3),
            pl.BlockSpec((PREP_GB, 2, _POW_ROWS, LANES), sel4),
        ],
        out_shape=[
            jax.ShapeDtypeStruct((lg, CW, CW), BF16),
            jax.ShapeDtypeStruct((lg, CW, 2 * LANES), BF16),
            jax.ShapeDtypeStruct((lg, 2 * LANES, CW), BF16),
            jax.ShapeDtypeStruct((lg, 2, _POW_ROWS, LANES), F32),
        ],
        scratch_shapes=[pltpu.VMEM((PREP_GB, CW, LANES), F32)] * 4,
        compiler_params=pltpu.CompilerParams(dimension_semantics=("arbitrary",)),
    )(lam, bt, c)


def _inproj_kernel(x_ref, g1_ref, win_ref, qg_ref, kg_ref, bdq_ref, bdk_ref,
                   q_ref, k_ref, v_ref, u_ref, wqkv_ref, wut_ref):
    @pl.when(pl.program_id(0) == 0)
    def _():
        wqkv_ref[...] = win_ref[:, :QKV_W].astype(BF16)
        wut_ref[...] = win_ref[:, QKV_W:].T.astype(BF16)

    def head_norm(t, bd_ref, gain):
        m = _dot((t * t).astype(BF16), bd_ref[...])
        return t * lax.rsqrt(m + EPS) * gain

    halves = [slice(r * TM_HALF, (r + 1) * TM_HALF) for r in range(TM_IN // TM_HALF)]
    hn = []
    for rows in halves:
        x = x_ref[rows, :]
        ms = jnp.mean(x * x, axis=-1, keepdims=True)
        hn.append((x * lax.rsqrt(ms + EPS) * g1_ref[...]).astype(BF16))
    z = [_dot(h, wqkv_ref[...]) for h in hn]
    ut = [_dot_nt(wut_ref[...], h) for h in hn]
    for rows, zz in zip(halves, z):
        q_ref[rows, :] = head_norm(zz[:, :ATT_W], bdq_ref, qg_ref[...]).astype(BF16)
        k_ref[rows, :] = head_norm(zz[:, ATT_W:ATT_W + KV_W], bdk_ref, kg_ref[...]).astype(BF16)
        v_ref[rows, :] = zz[:, ATT_W + KV_W:].astype(BF16)
    chunks = TM_IN // LANES
    u2d = u_ref.reshape(SSM_W * chunks, LANES)
    for r, utr in enumerate(ut):
        for jj in range(TM_HALF // LANES):
            j = r * (TM_HALF // LANES) + jj
            u2d[pl.ds(j, SSM_W, stride=chunks), :] = utr[:, jj * LANES:(jj + 1) * LANES]


def _inproj(x, g1, w_in, qg, kg, bdq, bdk, layer):
    t = x.shape[0]
    const = lambda i: (0, 0)
    sel = lambda i: (layer, 0, 0)
    return pl.pallas_call(
        _inproj_kernel,
        name="inproj",
        grid=(t // TM_IN,),
        in_specs=[
            pl.BlockSpec((TM_IN, D_MODEL), lambda i: (i, 0)),
            pl.BlockSpec((None, 1, D_MODEL), sel),
            pl.BlockSpec((None, D_MODEL, QKV_W + SSM_W), sel, pipeline_mode=pl.Buffered(1)),
            pl.BlockSpec((None, 1, ATT_W), sel),
            pl.BlockSpec((None, 1, KV_W), sel),
            pl.BlockSpec((ATT_W, ATT_W), const),
            pl.BlockSpec((KV_W, KV_W), const),
        ],
        out_specs=[
            pl.BlockSpec((TM_IN, ATT_W), lambda i: (i, 0)),
            pl.BlockSpec((TM_IN, KV_W), lambda i: (i, 0)),
            pl.BlockSpec((TM_IN, KV_W), lambda i: (i, 0)),
            pl.BlockSpec((SSM_W, TM_IN // LANES, LANES), lambda i: (0, i, 0)),
        ],
        out_shape=[
            jax.ShapeDtypeStruct((t, ATT_W), BF16),
            jax.ShapeDtypeStruct((t, KV_W), BF16),
            jax.ShapeDtypeStruct((t, KV_W), BF16),
            jax.ShapeDtypeStruct((SSM_W, t // LANES, LANES), F32),
        ],
        scratch_shapes=[pltpu.VMEM((D_MODEL, QKV_W), BF16), pltpu.VMEM((SSM_W, D_MODEL), BF16)],
        compiler_params=pltpu.CompilerParams(dimension_semantics=("arbitrary",),
                                             vmem_limit_bytes=VMEM_LIMIT),
    )(x, g1, w_in, qg, kg, bdq, bdk)


def _attn_kernel(sink_ref, q_ref, k_ref, v_ref, bias_ref, o_ref, s_scr, p_scr, *, n_blocks):
    n0 = pl.program_id(1) * ATT_NB
    prev = pl.multiple_of(jnp.maximum(n0 - 1, 0) * WINDOW, WINDOW)
    mid = pl.multiple_of(n0 * WINDOW, WINDOW)
    nxt = pl.multiple_of(jnp.minimum(n0 + ATT_NB, n_blocks - 1) * WINDOW, WINDOW)

    def window(ref):
        return jnp.concatenate([ref[0, pl.ds(prev, WINDOW), :], ref[0, pl.ds(mid, ATT_NB * WINDOW), :],
                                ref[0, pl.ds(nxt, WINDOW), :]], axis=0)

    kw = window(k_ref)
    vw = window(v_ref)
    kj = [kw[:, j * HEAD_DIM:(j + 1) * HEAD_DIM] for j in range(N_KV)]
    vj = [vw[:, j * HEAD_DIM:(j + 1) * HEAD_DIM] for j in range(N_KV)]

    def scores(b):
        n = n0 + b
        var = jnp.where(n == 0, 0, jnp.where(n == n_blocks - 1, 2, 1))
        qb = q_ref[b * WINDOW:(b + 1) * WINDOW, :]
        for h in range(N_HEADS):
            kh = kj[h // GQA][b * WINDOW:(b + 3) * WINDOW, :]
            s_scr[b % 2, h] = _dot_nt(qb[:, h * HEAD_DIM:(h + 1) * HEAD_DIM], kh) + bias_ref[var, h]

    def softmax(b):
        inv = []
        for h in range(N_HEADS):
            s = s_scr[b % 2, h]
            sk = sink_ref[h] * LOG2E
            m = jnp.maximum(jnp.max(s, axis=-1, keepdims=True), sk)
            p = jnp.exp2(s - m)
            inv.append(1.0 / (jnp.sum(p, axis=-1, keepdims=True) + jnp.exp2(sk - m)))
            p_scr[b % 2, h] = p.astype(BF16)
        return inv

    def weighted(b, inv):
        outs = []
        for h in range(N_HEADS):
            vh = vj[h // GQA][b * WINDOW:(b + 3) * WINDOW, :]
            outs.append(_dot(p_scr[b % 2, h], vh) * inv[h])
        o_ref[b * WINDOW:(b + 1) * WINDOW, :] = jnp.concatenate(outs, axis=-1).astype(BF16)

    scores(0)
    for b in range(ATT_NB):
        if b + 1 < ATT_NB:
            scores(b + 1)
        weighted(b, softmax(b))


def _attention(sink, q, k, v, bias, bsz, seq, layer):
    n_blocks = seq // WINDOW
    steps = n_blocks // ATT_NB
    rows = ATT_NB * WINDOW
    return pl.pallas_call(
        functools.partial(_attn_kernel, n_blocks=n_blocks),
        name="band_attn",
        grid=(bsz, steps),
        in_specs=[
            pl.BlockSpec(memory_space=pltpu.SMEM),
            pl.BlockSpec((rows, ATT_W), lambda b, n: (b * steps + n, 0)),
            pl.BlockSpec((1, seq, KV_W), lambda b, n: (b, 0, 0)),
            pl.BlockSpec((1, seq, KV_W), lambda b, n: (b, 0, 0)),
            pl.BlockSpec((3, N_HEADS, WINDOW, 3 * WINDOW), lambda b, n: (0, 0, 0, 0)),
        ],
        out_specs=pl.BlockSpec((rows, ATT_W), lambda b, n: (b * steps + n, 0)),
        out_shape=jax.ShapeDtypeStruct((bsz * seq, ATT_W), BF16),
        scratch_shapes=[pltpu.VMEM((2, N_HEADS, WINDOW, 3 * WINDOW), F32),
                        pltpu.VMEM((2, N_HEADS, WINDOW, 3 * WINDOW), BF16)],
        compiler_params=pltpu.CompilerParams(dimension_semantics=("arbitrary", "arbitrary"),
                                             vmem_limit_bytes=VMEM_LIMIT),
    )(sink[layer], q, k.reshape(bsz, seq, KV_W), v.reshape(bsz, seq, KV_W), bias)


def _attn_bias(seq):
    q_idx = jnp.arange(WINDOW)[:, None]
    c_idx = jnp.arange(3 * WINDOW)[None, :]
    dist = jnp.abs(q_idx - c_idx + WINDOW)
    slopes = jnp.exp2(-8.0 * jnp.arange(1, N_HEADS + 1, dtype=F32) / N_HEADS)
    alibi = -slopes[:, None, None] * dist.astype(F32)[None]
    in_band = (dist <= WINDOW)[None]
    blk = (c_idx // WINDOW)[None]
    variants = []
    for bad in (0, -1, 2):
        ok = in_band & (blk != bad)
        variants.append(jnp.where(ok, alibi * LOG2E, -1e30))
    return jnp.stack(variants).astype(F32)


def _swap_lane_blocks(vs):
    lane = lax.broadcasted_iota(I32, vs[0].shape, 1)
    lo_half = lane < 2 * SUB
    even_blk = (lane & SUB) == 0
    a0, a1, a2, a3 = vs
    c0 = jnp.where(lo_half, a0, pltpu.roll(a2, 2 * SUB, 1))
    c2 = jnp.where(lo_half, pltpu.roll(a0, 2 * SUB, 1), a2)
    c1 = jnp.where(lo_half, a1, pltpu.roll(a3, 2 * SUB, 1))
    c3 = jnp.where(lo_half, pltpu.roll(a1, 2 * SUB, 1), a3)
    b0 = jnp.where(even_blk, c0, pltpu.roll(c1, SUB, 1))
    b1 = jnp.where(even_blk, pltpu.roll(c0, LANES - SUB, 1), c1)
    b2 = jnp.where(even_blk, c2, pltpu.roll(c3, SUB, 1))
    b3 = jnp.where(even_blk, pltpu.roll(c2, LANES - SUB, 1), c3)
    return [b0, b1, b2, b3]


def _ssm_kernel(u_ref, t_ref, ms_ref, mo_ref, pw_ref, dsk_ref, y_ref, *, chunks_per_seq):
    n_grp = u_ref.shape[0] // SSM_CH
    nbc = u_ref.shape[1]
    is_fwd = lax.broadcasted_iota(I32, (nbc, LANES), 1) < SSM_P
    c_idx = lax.broadcasted_iota(I32, (nbc, LANES), 0) & (chunks_per_seq - 1)

    def cmul(x, a):
        return a[0] * x[0] - a[1] * x[1], a[0] * x[1] + a[1] * x[0]

    def cadd(x, y):
        return x[0] + y[0], x[1] + y[1]

    def shifted(x, dd):
        ok_f = jnp.logical_and(is_fwd, c_idx >= dd)
        ok_b = jnp.logical_and(jnp.logical_not(is_fwd), c_idx < chunks_per_seq - dd)
        return tuple(jnp.where(ok_f, pltpu.roll(t, dd, 0), jnp.where(ok_b, pltpu.roll(t, nbc - dd, 0), 0.0))
                     for t in x)

    def gather(g):
        slabs = [u_ref[g * SSM_CH + h] for h in range(SSM_CH)]
        cols = [_swap_lane_blocks(slabs[4 * q:4 * q + 4]) for q in range(SSM_CH // 4)]
        pieces = [jnp.concatenate([cols[q][i] for q in range(SSM_CH // 4)], axis=1)
                  for i in range(N_SUB)]
        return jnp.concatenate(pieces, axis=0)

    def entering_states(g, s_all):
        def prow(r):
            return pw_ref[g, 0, r:r + 1, :], pw_ref[g, 1, r:r + 1, :]

        def mirror(k):
            lo, hi = k * nbc, (N_SUB - 1 - k) * nbc
            return tuple(jnp.where(is_fwd, s_all[lo:lo + nbc, c:c + LANES], s_all[hi:hi + nbc, c:c + LANES])
                         for c in (0, LANES))

        a32 = prow(0)
        sk = [mirror(k) for k in range(N_SUB)]
        zero = jnp.zeros((nbc, LANES), F32)
        pk = [(zero, zero)]
        for k in range(N_SUB - 1):
            pk.append(cadd(cmul(pk[k], a32), sk[k]))
        x = cadd(cmul(pk[N_SUB - 1], a32), sk[N_SUB - 1])
        for kk in range(_N_SCAN_STEPS):
            x = cadd(x, cmul(shifted(x, 1 << kk), prow(3 + kk)))
        ent = shifted(x, 1)
        qk = [cadd(pk[k], cmul(ent, prow(k - 1)) if k > 0 else ent) for k in range(N_SUB)]
        xin = []
        for i in range(N_SUB):
            j = N_SUB - 1 - i
            xin.append(jnp.concatenate([jnp.where(is_fwd, qk[i][c], qk[j][c]) for c in (0, 1)], axis=1))
        return jnp.concatenate(xin, axis=0).astype(BF16)

    def scatter(g, act):
        for q in range(SSM_CH // 4):
            back = _swap_lane_blocks([act[i * nbc:(i + 1) * nbc, q * LANES:(q + 1) * LANES]
                                      for i in range(N_SUB)])
            for r in range(4):
                y_ref[g * SSM_CH + 4 * q + r] = back[r]

    grp = range(n_grp)
    ust = [gather(g) for g in grp]
    ub = [u.astype(BF16) for u in ust]
    z = [_dot(ub[g], t_ref[g]) for g in grp]
    s_all = [_dot(ub[g], ms_ref[g]) for g in grp]
    xin = [entering_states(g, s_all[g]) for g in grp]
    for g in grp:
        y = z[g] + _dot(xin[g], mo_ref[g]) + dsk_ref[g] * ust[g]
        scatter(g, jax.nn.gelu(y))


def _ssm(u3, t_all, ms_all, mo_all, pw_all, dsk_all, layer, chunks_per_seq):
    nbc = u3.shape[1]
    base = layer * (SSM_G // SSM_GB)
    sel3 = lambda g: (base + g, 0, 0)
    slab_spec = pl.BlockSpec((SSM_GB * SSM_CH, nbc, LANES), lambda g: (g, 0, 0))
    return pl.pallas_call(
        functools.partial(_ssm_kernel, chunks_per_seq=chunks_per_seq),
        name="ssm_mix",
        grid=(SSM_G // SSM_GB,),
        in_specs=[
            slab_spec,
            pl.BlockSpec((SSM_GB, CW, CW), sel3),
            pl.BlockSpec((SSM_GB, CW, 2 * LANES), sel3),
            pl.BlockSpec((SSM_GB, 2 * LANES, CW), sel3),
            pl.BlockSpec((SSM_GB, 2, _POW_ROWS, LANES), lambda g: (base + g, 0, 0, 0)),
            pl.BlockSpec((SSM_GB, 1, CW), sel3),
        ],
        out_specs=slab_spec,
        out_shape=jax.ShapeDtypeStruct((SSM_W, nbc, LANES), F32),
        compiler_params=pltpu.CompilerParams(dimension_semantics=("arbitrary",),
                                             vmem_limit_bytes=VMEM_LIMIT),
    )(u3, t_all, ms_all, mo_all, pw_all, dsk_all)


def _stream_cast(pairs, layer, stage_ref, sem_ref):
    slots, rows, cols = stage_ref.shape
    tiles = [(src, dst, r, c) for src, dst in pairs
             for r in range(0, dst.shape[0], rows) for c in range(0, dst.shape[1], cols)]

    def copy(i):
        src, _, r, c = tiles[i]
        return pltpu.make_async_copy(src.at[layer, pl.ds(r, rows), pl.ds(c, cols)],
                                     stage_ref.at[i % slots], sem_ref.at[i % slots])

    for i in range(min(slots - 1, len(tiles))):
        copy(i).start()
    for i, (_, dst, r, c) in enumerate(tiles):
        if i + slots - 1 < len(tiles):
            copy(i + slots - 1).start()
        copy(i).wait()
        dst[r:r + rows, c:c + cols] = stage_ref[i % slots].astype(BF16)


def _outffn_kernel(x_ref, att_ref, y_ref, g2_ref, wglu_hbm, wout_hbm, w1_hbm, w2_hbm, o_ref,
                   wglu_ref, wout_ref, w1_ref, w2_ref, stage_ref, stage_sem, *, layer):
    @pl.when(pl.program_id(0) == 0)
    def _():
        _stream_cast([(wglu_hbm, wglu_ref), (wout_hbm, wout_ref), (w1_hbm, w1_ref), (w2_hbm, w2_ref)],
                     layer, stage_ref, stage_sem)

    first = (pl.program_id(0) % (Y_CHUNKS * LANES // TM_FFN)) * (TM_FFN // LANES)
    y2d = y_ref.reshape(SSM_W * Y_CHUNKS, LANES)
    parts = [slice(r * FFN_PART, (r + 1) * FFN_PART) for r in range(TM_FFN // FFN_PART)]
    per = FFN_PART // LANES
    y = [jnp.concatenate([y2d[pl.ds(first + r * per + j, SSM_W, stride=Y_CHUNKS), :].T
                          for j in range(per)], axis=0).astype(BF16)
         for r in range(len(parts))]
    gl = [_dot(yy, wglu_ref[...]) for yy in y]
    ssm = [(g[:, :SSM_W] * jax.nn.sigmoid(g[:, SSM_W:])).astype(BF16) for g in gl]
    x = [x_ref[rows, :] + _dot(att_ref[rows, :], wout_ref[:ATT_W, :]) + _dot(s, wout_ref[ATT_W:, :])
         for rows, s in zip(parts, ssm)]
    hn = []
    for xx in x:
        ms = jnp.mean(xx * xx, axis=-1, keepdims=True)
        hn.append((xx * lax.rsqrt(ms + EPS) * g2_ref[...]).astype(BF16))
    hid = [jnp.maximum(_dot(h, w1_ref[...]), 0.0) for h in hn]
    for rows, xx, hh in zip(parts, x, hid):
        o_ref[rows, :] = xx + _dot((hh * hh).astype(BF16), w2_ref[...])


def _outffn(x, att, y3, g2, w_glu, w_out, w_ff1, w_ff2, layer):
    t = x.shape[0]
    hbm = pl.BlockSpec(memory_space=pl.ANY)
    return pl.pallas_call(
        functools.partial(_outffn_kernel, layer=layer),
        name="out_ffn",
        grid=(t // TM_FFN,),
        in_specs=[
            pl.BlockSpec((TM_FFN, D_MODEL), lambda i: (i, 0)),
            pl.BlockSpec((TM_FFN, ATT_W), lambda i: (i, 0)),
            pl.BlockSpec((SSM_W, Y_CHUNKS, LANES), lambda i: (0, i * TM_FFN // (Y_CHUNKS * LANES), 0)),
            pl.BlockSpec((None, 1, D_MODEL), lambda i: (layer, 0, 0)),
            hbm, hbm, hbm, hbm,
        ],
        out_specs=pl.BlockSpec((TM_FFN, D_MODEL), lambda i: (i, 0)),
        out_shape=jax.ShapeDtypeStruct((t, D_MODEL), F32),
        scratch_shapes=[
            pltpu.VMEM((SSM_W, 2 * SSM_W), BF16),
            pltpu.VMEM((D_MODEL, D_MODEL), BF16),
            pltpu.VMEM((D_MODEL, D_FF), BF16),
            pltpu.VMEM((D_FF, D_MODEL), BF16),
            pltpu.VMEM((STAGE_SLOTS, STAGE_ROWS, D_MODEL), F32),
            pltpu.SemaphoreType.DMA((STAGE_SLOTS,)),
        ],
        compiler_params=pltpu.CompilerParams(dimension_semantics=("arbitrary",),
                                             vmem_limit_bytes=VMEM_LIMIT),
    )(x, att, y3, g2, w_glu, w_out, w_ff1, w_ff2)


def _block_diag_mean(width):
    idx = jnp.arange(width) // HEAD_DIM
    return jnp.where(idx[:, None] == idx[None, :], 1.0 / HEAD_DIM, 0.0).astype(BF16)


def _prep_operands(lam_re, lam_im, log_dt, b_re, b_im, c_re, c_im, d_skip):
    depth = lam_re.shape[0]
    lg = depth * SSM_G
    both = lambda a: jnp.concatenate([a[:, 0], a[:, 1]], axis=-1)
    ldt = jnp.broadcast_to(log_dt[..., None], lam_re.shape)
    lam = jnp.stack([both(lam_re), both(lam_im), both(ldt)], axis=2).reshape(lg, 3, LANES)
    bt = jnp.stack([b_re, b_im], axis=2).transpose(0, 1, 2, 4, 3)
    bt = jnp.concatenate([bt, bt], axis=-1).reshape(lg, 2, SSM_CH, LANES)
    cc = jnp.stack([both(c_re), both(c_im)], axis=2).reshape(lg, 2, SSM_CH, LANES)
    dsk = jnp.repeat(d_skip.reshape(lg, 1, SSM_CH), SUB, axis=-1)
    return list(_ssm_prep(lam, bt, cc)) + [dsk]


def kernel(x, norm1, w_in, q_gain, k_gain, sink, lam_re, lam_im, log_dt, b_re, b_im, c_re, c_im,
           d_skip, w_glu, w_out, norm2, w_ff1, w_ff2):
    bsz, seq, _ = x.shape
    depth = norm1.shape[0]
    tokens = bsz * seq
    chunks_per_seq = seq // LANES
    assert seq % (ATT_NB * WINDOW) == 0 and tokens % TM_IN == 0 and tokens % TM_FFN == 0
    assert TM_IN == Y_CHUNKS * LANES and (Y_CHUNKS * LANES) % TM_FFN == 0
    assert chunks_per_seq == 1 << _N_SCAN_STEPS and max(_POW_N) == LANES * chunks_per_seq // 2

    t_all, ms_all, mo_all, pw_all, dsk_all = _prep_operands(lam_re, lam_im, log_dt, b_re, b_im,
                                                           c_re, c_im, d_skip)
    g1 = norm1[:, None, :]
    g2 = norm2[:, None, :]
    qg = (jnp.tile(q_gain, (1, N_HEADS)) * (LOG2E / math.sqrt(HEAD_DIM)))[:, None, :]
    kg = jnp.tile(k_gain, (1, N_KV))[:, None, :]
    bias = _attn_bias(seq)
    bdq = _block_diag_mean(ATT_W)
    bdk = _block_diag_mean(KV_W)

    xt = x.reshape(tokens, D_MODEL)
    for layer in range(depth):
        q, k, v, u3 = _inproj(xt, g1, w_in, qg, kg, bdq, bdk, layer)
        att = _attention(sink, q, k, v, bias, bsz, seq, layer)
        y3 = _ssm(u3, t_all, ms_all, mo_all, pw_all, dsk_all, layer, chunks_per_seq)
        xt = _outffn(xt, att, y3, g2, w_glu, w_out, w_ff1, w_ff2, layer)
    return xt.reshape(bsz, seq, D_MODEL)
```

```python
import functools
import math

import jax
import jax.numpy as jnp
from jax import lax
from jax.experimental import pallas as pl
from jax.experimental.pallas import tpu as pltpu

D_MODEL = 1024
N_HEADS = 8
N_KV = 2
GQA = N_HEADS // N_KV
HEAD_DIM = 64
ATT_W = N_HEADS * HEAD_DIM
KV_W = N_KV * HEAD_DIM
QKV_W = ATT_W + 2 * KV_W
WINDOW = 128
SSM_W = 512
SSM_CH = 16
SSM_G = SSM_W // SSM_CH
SSM_P = 64
D_FF = 4 * D_MODEL
EPS = 1e-6
LOG2E = math.log2(math.e)

LANES = 128
SUB = 32
N_SUB = LANES // SUB
CW = SSM_CH * SUB
VMEM_LIMIT = 56 * 1024 * 1024
STAGE_ROWS = 256
STAGE_SLOTS = 8

TM_IN = 1024
TM_HALF = 256
TM_FFN = 512
FFN_PART = 256
Y_CHUNKS = 8
ATT_NB = 4
SSM_GB = 2
PREP_GB = 4

_POW_N = (32, 64, 96, 128, 256, 512, 1024, 2048, 4096)
_POW_ROWS = 16
_N_SCAN_STEPS = 6

F32 = jnp.float32
BF16 = jnp.bfloat16
I32 = jnp.int32


def _dot(a, b):
    return jnp.dot(a, b, preferred_element_type=F32)


def _dot_nt(a, b, precision=None):
    return lax.dot_general(a, b, (((1,), (1,)), ((), ())), precision=precision,
                           preferred_element_type=F32)


def _prep_kernel(lam_ref, bt_ref, c_ref, t_ref, ms_ref, mo_ref, pw_ref,
                 g1r_ref, g1i_ref, mtr_ref, mti_ref):
    grp = range(lam_ref.shape[0])
    s_idx = lax.broadcasted_iota(I32, (SUB, LANES), 0)
    is_fwd = lax.broadcasted_iota(I32, (SUB, LANES), 1) < SSM_P
    fwd16 = lax.broadcasted_iota(I32, (SSM_CH, LANES), 1) < SSM_P
    e_t = jnp.where(is_fwd, s_idx, SUB - 1 - s_idx)

    def cmul(x, y):
        return x[0] * y[0] - x[1] * y[1], x[0] * y[1] + x[1] * y[0]

    def discretise(g):
        lr = lam_ref[g, 0:1, :]
        li = lam_ref[g, 1:2, :]
        dt = jnp.exp(lam_ref[g, 2:3, :])
        mag = jnp.exp(lr * dt)
        abr = mag * jnp.cos(li * dt)
        abi = mag * jnp.sin(li * dt)
        den = lr * lr + li * li
        zr = ((abr - 1.0) * lr + abi * li) / den
        zi = (abi * lr - (abr - 1.0) * li) / den
        btr = bt_ref[g, 0]
        bti = bt_ref[g, 1]
        sq = [(abr, abi)]
        for _ in range(max(_POW_N).bit_length() - 1):
            sq.append(cmul(sq[-1], sq[-1]))
        return sq, zr * btr - zi * bti, zr * bti + zi * btr

    def int_power(sq, e):
        acc = None
        for bit in range(SUB.bit_length() - 1):
            on = (e & (1 << bit)) != 0
            fac = (jnp.where(on, sq[bit][0], 1.0), jnp.where(on, sq[bit][1], 0.0))
            acc = fac if acc is None else cmul(acc, fac)
        return acc

    disc = [discretise(g) for g in grp]
    tpow = [int_power(disc[g][0], e_t) for g in grp]
    mspow = [int_power(disc[g][0], SUB - 1 - e_t) for g in grp]
    mopow = [cmul(tpow[g], disc[g][0][0]) for g in grp]

    for g in grp:
        _, bbr, bbi = disc[g]
        cr = c_ref[g, 0]
        ci = c_ref[g, 1]
        (tr, ti), (msr, msi), (mor, moi) = tpow[g], mspow[g], mopow[g]
        for h in range(SSM_CH):
            rows = slice(h * SUB, (h + 1) * SUB)
            b_r = bbr[h:h + 1, :]
            b_i = bbi[h:h + 1, :]
            ms_ref[g, rows, :LANES] = (msr * b_r - msi * b_i).astype(BF16)
            ms_ref[g, rows, LANES:] = (msr * b_i + msi * b_r).astype(BF16)
            c_r = cr[h:h + 1, :]
            c_i = ci[h:h + 1, :]
            g1r_ref[g, rows, :] = tr * c_r - ti * c_i
            g1i_ref[g, rows, :] = tr * c_i + ti * c_r
            mtr_ref[g, rows, :] = mor * c_r - moi * c_i
            mti_ref[g, rows, :] = -(mor * c_i + moi * c_r)
    for g in grp:
        mo_ref[g, :LANES, :] = mtr_ref[g].T.astype(BF16)
        mo_ref[g, LANES:, :] = mti_ref[g].T.astype(BF16)

    def split(t):
        hi = t.astype(BF16)
        return hi, (t - hi.astype(F32)).astype(BF16)

    def dot_nt_x3(a, b):
        return _dot_nt(a[0], b[0]) + (_dot_nt(a[0], b[1]) + _dot_nt(a[1], b[0]))

    kts = []
    for g in grp:
        _, bbr, bbi = disc[g]
        g1r = split(g1r_ref[g])
        g1i = split(g1i_ref[g])
        kts.append([dot_nt_x3(split(jnp.where(sel, bbr, 0.0)), g1r)
                    - dot_nt_x3(split(jnp.where(sel, bbi, 0.0)), g1i)
                    for sel in (fwd16, jnp.logical_not(fwd16))])

    for g in grp:
        sq = disc[g][0]
        pw_ref[g] = jnp.zeros((2, _POW_ROWS, LANES), F32)
        for r, n in enumerate(_POW_N):
            acc = None
            for bit in range(n.bit_length()):
                if n >> bit & 1:
                    acc = sq[bit] if acc is None else cmul(acc, sq[bit])
            pw_ref[g, 0, r:r + 1, :] = acc[0]
            pw_ref[g, 1, r:r + 1, :] = acc[1]

    s_lane = lax.broadcasted_iota(I32, (SUB, CW), 1) & (SUB - 1)
    s_row = lax.broadcasted_iota(I32, (SUB, CW), 0)
    fwd_mask = s_lane >= s_row
    tau0 = (lax.broadcasted_iota(I32, (SSM_CH, CW), 1) & (SUB - 1)) == 0
    for g in grp:
        kfd = kts[g][0] + jnp.where(tau0, pltpu.roll(kts[g][1], CW - (SUB - 1), 1), 0.0)
        for h in range(SSM_CH):
            kf = jnp.broadcast_to(kfd[h:h + 1, :], (SUB, CW))
            kb = jnp.broadcast_to(kts[g][1][h:h + 1, :], (SUB, CW))
            rf = pltpu.roll(kf, 0, 1, stride=1, stride_axis=0)
            rb = pltpu.roll(kb, CW - (SUB - 1), 1, stride=1, stride_axis=0)
            t_ref[g, h * SUB:(h + 1) * SUB, :] = jnp.where(fwd_mask, rf, rb).astype(BF16)


def _ssm_prep(lam, bt, c):
    lg = lam.shape[0]
    sel3 = lambda i: (i, 0, 0)
    sel4 = lambda i: (i, 0, 0, 0)
    return pl.pallas_call(
        _prep_kernel,
        name="ssm_prep",
        grid=(lg // PREP_GB,),
        in_specs=[
            pl.BlockSpec((PREP_GB, 3, LANES), sel3),
            pl.BlockSpec((PREP_GB, 2, SSM_CH, LANES), sel4),
            pl.BlockSpec((PREP_GB, 2, SSM_CH, LANES), sel4),
        ],
        out_specs=[
            pl.BlockSpec((PREP_GB, CW, CW), sel3),
            pl.BlockSpec((PREP_GB, CW, 2 * LANES), sel3),
            pl.BlockSpec((PREP_GB, 2 * LANES, CW), sel3),
            pl.BlockSpec((PREP_GB, 2, _POW_ROWS, LANES), sel4),
        ],
        out_shape=[
            jax.ShapeDtypeStruct((lg, CW, CW), BF16),
            jax.ShapeDtypeStruct((lg, CW, 2 * LANES), BF16),
            jax.ShapeDtypeStruct((lg, 2 * LANES, CW), BF16),
            jax.ShapeDtypeStruct((lg, 2, _POW_ROWS, LANES), F32),
        ],
        scratch_shapes=[pltpu.VMEM((PREP_GB, CW, LANES), F32)] * 4,
        compiler_params=pltpu.CompilerParams(dimension_semantics=("arbitrary",)),
    )(lam, bt, c)


def _inproj_kernel(x_ref, g1_ref, win_ref, qg_ref, kg_ref,
                   q_ref, k_ref, v_ref, u_ref, wqkv_ref, wut_ref):
    @pl.when(pl.program_id(0) == 0)
    def _():
        wqkv_ref[...] = win_ref[:, :QKV_W].astype(BF16)
        wut_ref[...] = win_ref[:, QKV_W:].T.astype(BF16)

    def head_norm(t, gain):
        low = lax.broadcasted_iota(I32, (t.shape[0], LANES), 1) < HEAD_DIM
        cols = []
        for c in range(t.shape[1] // LANES):
            t2 = t[:, c * LANES:(c + 1) * LANES]
            t2 = t2 * t2
            lo = jnp.sum(jnp.where(low, t2, 0.0), axis=-1, keepdims=True)
            hi = jnp.sum(jnp.where(low, 0.0, t2), axis=-1, keepdims=True)
            cols.append(jnp.where(low, lo, hi))
        m = jnp.concatenate(cols, axis=1) * (1.0 / HEAD_DIM)
        return t * lax.rsqrt(m + EPS) * gain

    halves = [slice(r * TM_HALF, (r + 1) * TM_HALF) for r in range(TM_IN // TM_HALF)]
    hn = []
    for rows in halves:
        x = x_ref[rows, :]
        ms = jnp.mean(x * x, axis=-1, keepdims=True)
        hn.append((x * lax.rsqrt(ms + EPS) * g1_ref[...]).astype(BF16))
    z = [_dot(h, wqkv_ref[...]) for h in hn]
    ut = [_dot_nt(wut_ref[...], h) for h in hn]
    for rows, zz in zip(halves, z):
        q_ref[rows, :] = head_norm(zz[:, :ATT_W], qg_ref[...]).astype(BF16)
        k_ref[rows, :] = head_norm(zz[:, ATT_W:ATT_W + KV_W], kg_ref[...]).astype(BF16)
        v_ref[rows, :] = zz[:, ATT_W + KV_W:].astype(BF16)
    chunks = TM_IN // LANES
    u2d = u_ref.reshape(SSM_W * chunks, LANES)
    for r, utr in enumerate(ut):
        for jj in range(TM_HALF // LANES):
            j = r * (TM_HALF // LANES) + jj
            u2d[pl.ds(j, SSM_W, stride=chunks), :] = utr[:, jj * LANES:(jj + 1) * LANES]


def _inproj(x, g1, w_in, qg, kg, layer):
    t = x.shape[0]
    sel = lambda i: (layer, 0, 0)
    return pl.pallas_call(
        _inproj_kernel,
        name="inproj",
        grid=(t // TM_IN,),
        in_specs=[
            pl.BlockSpec((TM_IN, D_MODEL), lambda i: (i, 0)),
            pl.BlockSpec((None, 1, D_MODEL), sel),
            pl.BlockSpec((None, D_MODEL, QKV_W + SSM_W), sel, pipeline_mode=pl.Buffered(1)),
            pl.BlockSpec((None, 1, ATT_W), sel),
            pl.BlockSpec((None, 1, KV_W), sel),
        ],
        out_specs=[
            pl.BlockSpec((TM_IN, ATT_W), lambda i: (i, 0)),
            pl.BlockSpec((TM_IN, KV_W), lambda i: (i, 0)),
            pl.BlockSpec((TM_IN, KV_W), lambda i: (i, 0)),
            pl.BlockSpec((SSM_W, TM_IN // LANES, LANES), lambda i: (0, i, 0)),
        ],
        out_shape=[
            jax.ShapeDtypeStruct((t, ATT_W), BF16),
            jax.ShapeDtypeStruct((t, KV_W), BF16),
            jax.ShapeDtypeStruct((t, KV_W), BF16),
            jax.ShapeDtypeStruct((SSM_W, t // LANES, LANES), F32),
        ],
        scratch_shapes=[pltpu.VMEM((D_MODEL, QKV_W), BF16), pltpu.VMEM((SSM_W, D_MODEL), BF16)],
        compiler_params=pltpu.CompilerParams(dimension_semantics=("arbitrary",),
                                             vmem_limit_bytes=VMEM_LIMIT),
    )(x, g1, w_in, qg, kg)


def _attn_kernel(sink_ref, q_ref, k_ref, v_ref, bias_ref, o_ref, s_scr, p_scr, *, n_blocks):
    n0 = pl.program_id(1) * ATT_NB
    prev = pl.multiple_of(jnp.maximum(n0 - 1, 0) * WINDOW, WINDOW)
    mid = pl.multiple_of(n0 * WINDOW, WINDOW)
    nxt = pl.multiple_of(jnp.minimum(n0 + ATT_NB, n_blocks - 1) * WINDOW, WINDOW)

    def window(ref):
        return jnp.concatenate([ref[0, pl.ds(prev, WINDOW), :], ref[0, pl.ds(mid, ATT_NB * WINDOW), :],
                                ref[0, pl.ds(nxt, WINDOW), :]], axis=0)

    kw = window(k_ref)
    vw = window(v_ref)
    kj = [kw[:, j * HEAD_DIM:(j + 1) * HEAD_DIM] for j in range(N_KV)]
    vj = [vw[:, j * HEAD_DIM:(j + 1) * HEAD_DIM] for j in range(N_KV)]

    def scores(b):
        n = n0 + b
        var = jnp.where(n == 0, 0, jnp.where(n == n_blocks - 1, 2, 1))
        qb = q_ref[b * WINDOW:(b + 1) * WINDOW, :]
        for h in range(N_HEADS):
            kh = kj[h // GQA][b * WINDOW:(b + 3) * WINDOW, :]
            s_scr[b % 2, h] = _dot_nt(qb[:, h * HEAD_DIM:(h + 1) * HEAD_DIM], kh) + bias_ref[var, h]

    def softmax(b):
        inv = []
        for h in range(N_HEADS):
            s = s_scr[b % 2, h]
            sk = sink_ref[h] * LOG2E
            m = jnp.maximum(jnp.max(s, axis=-1, keepdims=True), sk)
            p = jnp.exp2(s - m)
            inv.append(1.0 / (jnp.sum(p, axis=-1, keepdims=True) + jnp.exp2(sk - m)))
            p_scr[b % 2, h] = p.astype(BF16)
        return inv

    def weighted(b, inv):
        outs = []
        for h in range(N_HEADS):
            vh = vj[h // GQA][b * WINDOW:(b + 3) * WINDOW, :]
            outs.append(_dot(p_scr[b % 2, h], vh) * inv[h])
        o_ref[b * WINDOW:(b + 1) * WINDOW, :] = jnp.concatenate(outs, axis=-1).astype(BF16)

    scores(0)
    for b in range(ATT_NB):
        if b + 1 < ATT_NB:
            scores(b + 1)
        weighted(b, softmax(b))


def _attention(sink, q, k, v, bias, bsz, seq, layer):
    n_blocks = seq // WINDOW
    steps = n_blocks // ATT_NB
    rows = ATT_NB * WINDOW
    return pl.pallas_call(
        functools.partial(_attn_kernel, n_blocks=n_blocks),
        name="band_attn",
        grid=(bsz, steps),
        in_specs=[
            pl.BlockSpec(memory_space=pltpu.SMEM),
            pl.BlockSpec((rows, ATT_W), lambda b, n: (b * steps + n, 0)),
            pl.BlockSpec((1, seq, KV_W), lambda b, n: (b, 0, 0)),
            pl.BlockSpec((1, seq, KV_W), lambda b, n: (b, 0, 0)),
            pl.BlockSpec((3, N_HEADS, WINDOW, 3 * WINDOW), lambda b, n: (0, 0, 0, 0)),
        ],
        out_specs=pl.BlockSpec((rows, ATT_W), lambda b, n: (b * steps + n, 0)),
        out_shape=jax.ShapeDtypeStruct((bsz * seq, ATT_W), BF16),
        scratch_shapes=[pltpu.VMEM((2, N_HEADS, WINDOW, 3 * WINDOW), F32),
                        pltpu.VMEM((2, N_HEADS, WINDOW, 3 * WINDOW), BF16)],
        compiler_params=pltpu.CompilerParams(dimension_semantics=("arbitrary", "arbitrary"),
                                             vmem_limit_bytes=VMEM_LIMIT),
    )(sink[layer], q, k.reshape(bsz, seq, KV_W), v.reshape(bsz, seq, KV_W), bias)


def _attn_bias(seq):
    q_idx = jnp.arange(WINDOW)[:, None]
    c_idx = jnp.arange(3 * WINDOW)[None, :]
    dist = jnp.abs(q_idx - c_idx + WINDOW)
    slopes = jnp.exp2(-8.0 * jnp.arange(1, N_HEADS + 1, dtype=F32) / N_HEADS)
    alibi = -slopes[:, None, None] * dist.astype(F32)[None]
    in_band = (dist <= WINDOW)[None]
    blk = (c_idx // WINDOW)[None]
    variants = []
    for bad in (0, -1, 2):
        ok = in_band & (blk != bad)
        variants.append(jnp.where(ok, alibi * LOG2E, -1e30))
    return jnp.stack(variants).astype(F32)


def _swap_lane_blocks(vs):
    lane = lax.broadcasted_iota(I32, vs[0].shape, 1)
    lo_half = lane < 2 * SUB
    even_blk = (lane & SUB) == 0
    a0, a1, a2, a3 = vs
    c0 = jnp.where(lo_half, a0, pltpu.roll(a2, 2 * SUB, 1))
    c2 = jnp.where(lo_half, pltpu.roll(a0, 2 * SUB, 1), a2)
    c1 = jnp.where(lo_half, a1, pltpu.roll(a3, 2 * SUB, 1))
    c3 = jnp.where(lo_half, pltpu.roll(a1, 2 * SUB, 1), a3)
    b0 = jnp.where(even_blk, c0, pltpu.roll(c1, SUB, 1))
    b1 = jnp.where(even_blk, pltpu.roll(c0, LANES - SUB, 1), c1)
    b2 = jnp.where(even_blk, c2, pltpu.roll(c3, SUB, 1))
    b3 = jnp.where(even_blk, pltpu.roll(c2, LANES - SUB, 1), c3)
    return [b0, b1, b2, b3]


def _ssm_kernel(u_ref, t_ref, ms_ref, mo_ref, pw_ref, dsk_ref, y_ref, *, chunks_per_seq):
    n_grp = u_ref.shape[0] // SSM_CH
    nbc = u_ref.shape[1]
    is_fwd = lax.broadcasted_iota(I32, (nbc, LANES), 1) < SSM_P
    c_idx = lax.broadcasted_iota(I32, (nbc, LANES), 0) & (chunks_per_seq - 1)

    def cmul(x, a):
        return a[0] * x[0] - a[1] * x[1], a[0] * x[1] + a[1] * x[0]

    def cadd(x, y):
        return x[0] + y[0], x[1] + y[1]

    def shifted(x, dd):
        ok_f = jnp.logical_and(is_fwd, c_idx >= dd)
        ok_b = jnp.logical_and(jnp.logical_not(is_fwd), c_idx < chunks_per_seq - dd)
        return tuple(jnp.where(ok_f, pltpu.roll(t, dd, 0), jnp.where(ok_b, pltpu.roll(t, nbc - dd, 0), 0.0))
                     for t in x)

    def gather(g):
        slabs = [u_ref[g * SSM_CH + h] for h in range(SSM_CH)]
        cols = [_swap_lane_blocks(slabs[4 * q:4 * q + 4]) for q in range(SSM_CH // 4)]
        pieces = [jnp.concatenate([cols[q][i] for q in range(SSM_CH // 4)], axis=1)
                  for i in range(N_SUB)]
        return jnp.concatenate(pieces, axis=0)

    def entering_states(g, s_all):
        def prow(r):
            return pw_ref[g, 0, r:r + 1, :], pw_ref[g, 1, r:r + 1, :]

        def mirror(k):
            lo, hi = k * nbc, (N_SUB - 1 - k) * nbc
            return tuple(jnp.where(is_fwd, s_all[lo:lo + nbc, c:c + LANES], s_all[hi:hi + nbc, c:c + LANES])
                         for c in (0, LANES))

        a32 = prow(0)
        sk = [mirror(k) for k in range(N_SUB)]
        zero = jnp.zeros((nbc, LANES), F32)
        pk = [(zero, zero)]
        for k in range(N_SUB - 1):
            pk.append(cadd(cmul(pk[k], a32), sk[k]))
        x = cadd(cmul(pk[N_SUB - 1], a32), sk[N_SUB - 1])
        for kk in range(_N_SCAN_STEPS):
            x = cadd(x, cmul(shifted(x, 1 << kk), prow(3 + kk)))
        ent = shifted(x, 1)
        qk = [cadd(pk[k], cmul(ent, prow(k - 1)) if k > 0 else ent) for k in range(N_SUB)]
        xin = []
        for i in range(N_SUB):
            j = N_SUB - 1 - i
            xin.append(jnp.concatenate([jnp.where(is_fwd, qk[i][c], qk[j][c]) for c in (0, 1)], axis=1))
        return jnp.concatenate(xin, axis=0).astype(BF16)

    def scatter(g, act):
        for q in range(SSM_CH // 4):
            back = _swap_lane_blocks([act[i * nbc:(i + 1) * nbc, q * LANES:(q + 1) * LANES]
                                      for i in range(N_SUB)])
            for r in range(4):
                y_ref[g * SSM_CH + 4 * q + r] = back[r]

    grp = range(n_grp)
    ust = [gather(g) for g in grp]
    ub = [u.astype(BF16) for u in ust]
    z = [_dot(ub[g], t_ref[g]) for g in grp]
    s_all = [_dot(ub[g], ms_ref[g]) for g in grp]
    xin = [entering_states(g, s_all[g]) for g in grp]
    for g in grp:
        y = z[g] + _dot(xin[g], mo_ref[g]) + dsk_ref[g] * ust[g]
        scatter(g, jax.nn.gelu(y))


def _ssm(u3, t_all, ms_all, mo_all, pw_all, dsk_all, layer, chunks_per_seq):
    nbc = u3.shape[1]
    base = layer * (SSM_G // SSM_GB)
    sel3 = lambda g: (base + g, 0, 0)
    slab_spec = pl.BlockSpec((SSM_GB * SSM_CH, nbc, LANES), lambda g: (g, 0, 0))
    return pl.pallas_call(
        functools.partial(_ssm_kernel, chunks_per_seq=chunks_per_seq),
        name="ssm_mix",
        grid=(SSM_G // SSM_GB,),
        in_specs=[
            slab_spec,
            pl.BlockSpec((SSM_GB, CW, CW), sel3),
            pl.BlockSpec((SSM_GB, CW, 2 * LANES), sel3),
            pl.BlockSpec((SSM_GB, 2 * LANES, CW), sel3),
            pl.BlockSpec((SSM_GB, 2, _POW_ROWS, LANES), lambda g: (base + g, 0, 0, 0)),
            pl.BlockSpec((SSM_GB, 1, CW), sel3),
        ],
        out_specs=slab_spec,
        out_shape=jax.ShapeDtypeStruct((SSM_W, nbc, LANES), F32),
        compiler_params=pltpu.CompilerParams(dimension_semantics=("arbitrary",),
                                             vmem_limit_bytes=VMEM_LIMIT),
    )(u3, t_all, ms_all, mo_all, pw_all, dsk_all)


def _stream_cast(pairs, layer, stage_ref, sem_ref):
    slots, rows, cols = stage_ref.shape
    tiles = [(src, dst, r, c) for src, dst in pairs
             for r in range(0, dst.shape[0], rows) for c in range(0, dst.shape[1], cols)]

    def copy(i):
        src, _, r, c = tiles[i]
        return pltpu.make_async_copy(src.at[layer, pl.ds(r, rows), pl.ds(c, cols)],
                                     stage_ref.at[i % slots], sem_ref.at[i % slots])

    for i in range(min(slots - 1, len(tiles))):
        copy(i).start()
    for i, (_, dst, r, c) in enumerate(tiles):
        if i + slots - 1 < len(tiles):
            copy(i + slots - 1).start()
        copy(i).wait()
        dst[r:r + rows, c:c + cols] = stage_ref[i % slots].astype(BF16)


def _outffn_kernel(x_ref, att_ref, y_ref, g2_ref, wglu_hbm, wout_hbm, w1_hbm, w2_hbm, o_ref,
                   wglu_ref, wout_ref, w1_ref, w2_ref, stage_ref, stage_sem, *, layer):
    @pl.when(pl.program_id(0) == 0)
    def _():
        _stream_cast([(wglu_hbm, wglu_ref), (wout_hbm, wout_ref), (w1_hbm, w1_ref), (w2_hbm, w2_ref)],
                     layer, stage_ref, stage_sem)

    first = (pl.program_id(0) % (Y_CHUNKS * LANES // TM_FFN)) * (TM_FFN // LANES)
    y2d = y_ref.reshape(SSM_W * Y_CHUNKS, LANES)
    parts = [slice(r * FFN_PART, (r + 1) * FFN_PART) for r in range(TM_FFN // FFN_PART)]
    per = FFN_PART // LANES
    y = [jnp.concatenate([y2d[pl.ds(first + r * per + j, SSM_W, stride=Y_CHUNKS), :].T
                          for j in range(per)], axis=0).astype(BF16)
         for r in range(len(parts))]
    gl = [_dot(yy, wglu_ref[...]) for yy in y]
    ssm = [(g[:, :SSM_W] * jax.nn.sigmoid(g[:, SSM_W:])).astype(BF16) for g in gl]
    x = [x_ref[rows, :] + _dot(att_ref[rows, :], wout_ref[:ATT_W, :]) + _dot(s, wout_ref[ATT_W:, :])
         for rows, s in zip(parts, ssm)]
    hn = []
    for xx in x:
        ms = jnp.mean(xx * xx, axis=-1, keepdims=True)
        hn.append((xx * lax.rsqrt(ms + EPS) * g2_ref[...]).astype(BF16))
    hid = [jnp.maximum(_dot(h, w1_ref[...]), 0.0) for h in hn]
    for rows, xx, hh in zip(parts, x, hid):
        o_ref[rows, :] = xx + _dot((hh * hh).astype(BF16), w2_ref[...])


def _outffn(x, att, y3, g2, w_glu, w_out, w_ff1, w_ff2, layer):
    t = x.shape[0]
    hbm = pl.BlockSpec(memory_space=pl.ANY)
    return pl.pallas_call(
        functools.partial(_outffn_kernel, layer=layer),
        name="out_ffn",
        grid=(t // TM_FFN,),
        in_specs=[
            pl.BlockSpec((TM_FFN, D_MODEL), lambda i: (i, 0)),
            pl.BlockSpec((TM_FFN, ATT_W), lambda i: (i, 0)),
            pl.BlockSpec((SSM_W, Y_CHUNKS, LANES), lambda i: (0, i * TM_FFN // (Y_CHUNKS * LANES), 0)),
            pl.BlockSpec((None, 1, D_MODEL), lambda i: (layer, 0, 0)),
            hbm, hbm, hbm, hbm,
        ],
        out_specs=pl.BlockSpec((TM_FFN, D_MODEL), lambda i: (i, 0)),
        out_shape=jax.ShapeDtypeStruct((t, D_MODEL), F32),
        scratch_shapes=[
            pltpu.VMEM((SSM_W, 2 * SSM_W), BF16),
            pltpu.VMEM((D_MODEL, D_MODEL), BF16),
            pltpu.VMEM((D_MODEL, D_FF), BF16),
            pltpu.VMEM((D_FF, D_MODEL), BF16),
            pltpu.VMEM((STAGE_SLOTS, STAGE_ROWS, D_MODEL), F32),
            pltpu.SemaphoreType.DMA((STAGE_SLOTS,)),
        ],
        compiler_params=pltpu.CompilerParams(dimension_semantics=("arbitrary",),
                                             vmem_limit_bytes=VMEM_LIMIT),
    )(x, att, y3, g2, w_glu, w_out, w_ff1, w_ff2)


def _prep_operands(lam_re, lam_im, log_dt, b_re, b_im, c_re, c_im, d_skip):
    depth = lam_re.shape[0]
    lg = depth * SSM_G
    both = lambda a: jnp.concatenate([a[:, 0], a[:, 1]], axis=-1)
    ldt = jnp.broadcast_to(log_dt[..., None], lam_re.shape)
    lam = jnp.stack([both(lam_re), both(lam_im), both(ldt)], axis=2).reshape(lg, 3, LANES)
    bt = jnp.stack([b_re, b_im], axis=2).transpose(0, 1, 2, 4, 3)
    bt = jnp.concatenate([bt, bt], axis=-1).reshape(lg, 2, SSM_CH, LANES)
    cc = jnp.stack([both(c_re), both(c_im)], axis=2).reshape(lg, 2, SSM_CH, LANES)
    dsk = jnp.repeat(d_skip.reshape(lg, 1, SSM_CH), SUB, axis=-1)
    return list(_ssm_prep(lam, bt, cc)) + [dsk]


def kernel(x, norm1, w_in, q_gain, k_gain, sink, lam_re, lam_im, log_dt, b_re, b_im, c_re, c_im,
           d_skip, w_glu, w_out, norm2, w_ff1, w_ff2):
    bsz, seq, _ = x.shape
    depth = norm1.shape[0]
    tokens = bsz * seq
    chunks_per_seq = seq // LANES
    assert seq % (ATT_NB * WINDOW) == 0 and tokens % TM_IN == 0 and tokens % TM_FFN == 0
    assert TM_IN == Y_CHUNKS * LANES and (Y_CHUNKS * LANES) % TM_FFN == 0
    assert chunks_per_seq == 1 << _N_SCAN_STEPS and max(_POW_N) == LANES * chunks_per_seq // 2

    t_all, ms_all, mo_all, pw_all, dsk_all = _prep_operands(lam_re, lam_im, log_dt, b_re, b_im,
                                                           c_re, c_im, d_skip)
    g1 = norm1[:, None, :]
    g2 = norm2[:, None, :]
    qg = (jnp.tile(q_gain, (1, N_HEADS)) * (LOG2E / math.sqrt(HEAD_DIM)))[:, None, :]
    kg = jnp.tile(k_gain, (1, N_KV))[:, None, :]
    bias = _attn_bias(seq)

    xt = x.reshape(tokens, D_MODEL)
    for layer in range(depth):
        q, k, v, u3 = _inproj(xt, g1, w_in, qg, kg, layer)
        att = _attention(sink, q, k, v, bias, bsz, seq, layer)
        y3 = _ssm(u3, t_all, ms_all, mo_all, pw_all, dsk_all, layer, chunks_per_seq)
        xt = _outffn(xt, att, y3, g2, w_glu, w_out, w_ff1, w_ff2, layer)
    return xt.reshape(bsz, seq, D_MODEL)
```

```python
import functools
import math

import jax
import jax.numpy as jnp
from jax import lax
from jax.experimental import pallas as pl
from jax.experimental.pallas import tpu as pltpu

D_MODEL = 1024
N_HEADS = 8
N_KV = 2
GQA = N_HEADS // N_KV
HEAD_DIM = 64
ATT_W = N_HEADS * HEAD_DIM
KV_W = N_KV * HEAD_DIM
QKV_W = ATT_W + 2 * KV_W
WINDOW = 128
SSM_W = 512
SSM_CH = 16
SSM_G = SSM_W // SSM_CH
SSM_P = 64
D_FF = 4 * D_MODEL
EPS = 1e-6
LOG2E = math.log2(math.e)

LANES = 128
SUB = 32
N_SUB = LANES // SUB
CW = SSM_CH * SUB
VMEM_LIMIT = 56 * 1024 * 1024
STAGE_ROWS = 256
STAGE_SLOTS = 8

TM_IN = 1024
TM_HALF = 256
TM_FFN = 512
FFN_PART = 256
Y_CHUNKS = 8
ATT_NB = 4
SSM_GB = 2
PREP_GB = 4

_POW_N = (32, 64, 96, 128, 256, 512, 1024, 2048, 4096)
_POW_ROWS = 16
_N_SCAN_STEPS = 6

F32 = jnp.float32
BF16 = jnp.bfloat16
I32 = jnp.int32


def _dot(a, b):
    return jnp.dot(a, b, preferred_element_type=F32)


def _dot_nt(a, b, precision=None):
    return lax.dot_general(a, b, (((1,), (1,)), ((), ())), precision=precision,
                           preferred_element_type=F32)


def _prep_kernel(lam_ref, bt_ref, c_ref, t_ref, ms_ref, mo_ref, pw_ref,
                 g1r_ref, g1i_ref, mtr_ref, mti_ref):
    grp = range(lam_ref.shape[0])
    s_idx = lax.broadcasted_iota(I32, (SUB, LANES), 0)
    is_fwd = lax.broadcasted_iota(I32, (SUB, LANES), 1) < SSM_P
    fwd16 = lax.broadcasted_iota(I32, (SSM_CH, LANES), 1) < SSM_P
    e_t = jnp.where(is_fwd, s_idx, SUB - 1 - s_idx)

    def cmul(x, y):
        return x[0] * y[0] - x[1] * y[1], x[0] * y[1] + x[1] * y[0]

    def discretise(g):
        lr = lam_ref[g, 0:1, :]
        li = lam_ref[g, 1:2, :]
        dt = jnp.exp(lam_ref[g, 2:3, :])
        mag = jnp.exp(lr * dt)
        abr = mag * jnp.cos(li * dt)
        abi = mag * jnp.sin(li * dt)
        den = lr * lr + li * li
        zr = ((abr - 1.0) * lr + abi * li) / den
        zi = (abi * lr - (abr - 1.0) * li) / den
        btr = bt_ref[g, 0]
        bti = bt_ref[g, 1]
        sq = [(abr, abi)]
        for _ in range(max(_POW_N).bit_length() - 1):
            sq.append(cmul(sq[-1], sq[-1]))
        return sq, zr * btr - zi * bti, zr * bti + zi * btr

    def int_power(sq, e):
        acc = None
        for bit in range(SUB.bit_length() - 1):
            on = (e & (1 << bit)) != 0
            fac = (jnp.where(on, sq[bit][0], 1.0), jnp.where(on, sq[bit][1], 0.0))
            acc = fac if acc is None else cmul(acc, fac)
        return acc

    disc = [discretise(g) for g in grp]
    tpow = [int_power(disc[g][0], e_t) for g in grp]
    mspow = [int_power(disc[g][0], SUB - 1 - e_t) for g in grp]
    mopow = [cmul(tpow[g], disc[g][0][0]) for g in grp]

    for g in grp:
        _, bbr, bbi = disc[g]
        cr = c_ref[g, 0]
        ci = c_ref[g, 1]
        (tr, ti), (msr, msi), (mor, moi) = tpow[g], mspow[g], mopow[g]
        for h in range(SSM_CH):
            rows = slice(h * SUB, (h + 1) * SUB)
            b_r = bbr[h:h + 1, :]
            b_i = bbi[h:h + 1, :]
            ms_ref[g, rows, :LANES] = (msr * b_r - msi * b_i).astype(BF16)
            ms_ref[g, rows, LANES:] = (msr * b_i + msi * b_r).astype(BF16)
            c_r = cr[h:h + 1, :]
            c_i = ci[h:h + 1, :]
            g1r_ref[g, rows, :] = tr * c_r - ti * c_i
            g1i_ref[g, rows, :] = tr * c_i + ti * c_r
            mtr_ref[g, rows, :] = mor * c_r - moi * c_i
            mti_ref[g, rows, :] = -(mor * c_i + moi * c_r)
    for g in grp:
        mo_ref[g, :LANES, :] = mtr_ref[g].T.astype(BF16)
        mo_ref[g, LANES:, :] = mti_ref[g].T.astype(BF16)

    def split(t):
        hi = t.astype(BF16)
        return hi, (t - hi.astype(F32)).astype(BF16)

    def dot_nt_x3(a, b):
        return _dot_nt(a[0], b[0]) + (_dot_nt(a[0], b[1]) + _dot_nt(a[1], b[0]))

    kts = []
    for g in grp:
        _, bbr, bbi = disc[g]
        g1r = split(g1r_ref[g])
        g1i = split(g1i_ref[g])
        kts.append([dot_nt_x3(split(jnp.where(sel, bbr, 0.0)), g1r)
                    - dot_nt_x3(split(jnp.where(sel, bbi, 0.0)), g1i)
                    for sel in (fwd16, jnp.logical_not(fwd16))])

    for g in grp:
        sq = disc[g][0]
        pw_ref[g] = jnp.zeros((2, _POW_ROWS, LANES), F32)
        for r, n in enumerate(_POW_N):
            acc = None
            for bit in range(n.bit_length()):
                if n >> bit & 1:
                    acc = sq[bit] if acc is None else cmul(acc, sq[bit])
            pw_ref[g, 0, r:r + 1, :] = acc[0]
            pw_ref[g, 1, r:r + 1, :] = acc[1]

    s_lane = lax.broadcasted_iota(I32, (SUB, CW), 1) & (SUB - 1)
    s_row = lax.broadcasted_iota(I32, (SUB, CW), 0)
    fwd_mask = s_lane >= s_row
    tau0 = (lax.broadcasted_iota(I32, (SSM_CH, CW), 1) & (SUB - 1)) == 0
    for g in grp:
        kfd = kts[g][0] + jnp.where(tau0, pltpu.roll(kts[g][1], CW - (SUB - 1), 1), 0.0)
        for h in range(SSM_CH):
            kf = jnp.broadcast_to(kfd[h:h + 1, :], (SUB, CW))
            kb = jnp.broadcast_to(kts[g][1][h:h + 1, :], (SUB, CW))
            rf = pltpu.roll(kf, 0, 1, stride=1, stride_axis=0)
            rb = pltpu.roll(kb, CW - (SUB - 1), 1, stride=1, stride_axis=0)
            t_ref[g, h * SUB:(h + 1) * SUB, :] = jnp.where(fwd_mask, rf, rb).astype(BF16)


def _ssm_prep(lam, bt, c):
    lg = lam.shape[0]
    sel3 = lambda i: (i, 0, 0)
    sel4 = lambda i: (i, 0, 0, 0)
    return pl.pallas_call(
        _prep_kernel,
        name="ssm_prep",
        grid=(lg // PREP_GB,),
        in_specs=[
            pl.BlockSpec((PREP_GB, 3, LANES), sel3),
            pl.BlockSpec((PREP_GB, 2, SSM_CH, LANES), sel4),
            pl.BlockSpec((PREP_GB, 2, SSM_CH, LANES), sel4),
        ],
        out_specs=[
            pl.BlockSpec((PREP_GB, CW, CW), sel3),
            pl.BlockSpec((PREP_GB, CW, 2 * LANES), sel3),
            pl.BlockSpec((PREP_GB, 2 * LANES, CW), sel3),
            pl.BlockSpec((PREP_GB, 2, _POW_ROWS, LANES), sel4),
        ],
        out_shape=[
            jax.ShapeDtypeStruct((lg, CW, CW), BF16),
            jax.ShapeDtypeStruct((lg, CW, 2 * LANES), BF16),
            jax.ShapeDtypeStruct((lg, 2 * LANES, CW), BF16),
            jax.ShapeDtypeStruct((lg, 2, _POW_ROWS, LANES), F32),
        ],
        scratch_shapes=[pltpu.VMEM((PREP_GB, CW, LANES), F32)] * 4,
        compiler_params=pltpu.CompilerParams(dimension_semantics=("arbitrary",)),
    )(lam, bt, c)


def _inproj_kernel(x_ref, g1_ref, win_ref, qg_ref, kg_ref,
                   q_ref, k_ref, v_ref, vs_ref, u_ref, wqkv_ref, wut_ref):
    @pl.when(pl.program_id(0) == 0)
    def _():
        wqkv_ref[...] = win_ref[:, :QKV_W].astype(BF16)
        wut_ref[...] = win_ref[:, QKV_W:].T.astype(BF16)

    def head_norm(t, gain):
        low = lax.broadcasted_iota(I32, (t.shape[0], LANES), 1) < HEAD_DIM
        cols = []
        for c in range(t.shape[1] // LANES):
            t2 = t[:, c * LANES:(c + 1) * LANES]
            t2 = t2 * t2
            lo = jnp.sum(jnp.where(low, t2, 0.0), axis=-1, keepdims=True)
            hi = jnp.sum(jnp.where(low, 0.0, t2), axis=-1, keepdims=True)
            cols.append(jnp.where(low, lo, hi))
        m = jnp.concatenate(cols, axis=1) * (1.0 / HEAD_DIM)
        return t * lax.rsqrt(m + EPS) * gain

    halves = [slice(r * TM_HALF, (r + 1) * TM_HALF) for r in range(TM_IN // TM_HALF)]
    hn = []
    for rows in halves:
        x = x_ref[rows, :]
        ms = jnp.mean(x * x, axis=-1, keepdims=True)
        hn.append((x * lax.rsqrt(ms + EPS) * g1_ref[...]).astype(BF16))
    z = [_dot(h, wqkv_ref[...]) for h in hn]
    ut = [_dot_nt(wut_ref[...], h) for h in hn]
    for rows, zz in zip(halves, z):
        q_ref[rows, :] = head_norm(zz[:, :ATT_W], qg_ref[...]).astype(BF16)
        k_ref[rows, :] = head_norm(zz[:, ATT_W:ATT_W + KV_W], kg_ref[...]).astype(BF16)
        vv = zz[:, ATT_W + KV_W:]
        v_ref[rows, :] = vv.astype(BF16)
        vs_ref[rows, :] = pltpu.roll(vv, HEAD_DIM, 1).astype(BF16)
    chunks = TM_IN // LANES
    u2d = u_ref.reshape(SSM_W * chunks, LANES)
    for r, utr in enumerate(ut):
        for jj in range(TM_HALF // LANES):
            j = r * (TM_HALF // LANES) + jj
            u2d[pl.ds(j, SSM_W, stride=chunks), :] = utr[:, jj * LANES:(jj + 1) * LANES]


def _inproj(x, g1, w_in, qg, kg, layer):
    t = x.shape[0]
    sel = lambda i: (layer, 0, 0)
    return pl.pallas_call(
        _inproj_kernel,
        name="inproj",
        grid=(t // TM_IN,),
        in_specs=[
            pl.BlockSpec((TM_IN, D_MODEL), lambda i: (i, 0)),
            pl.BlockSpec((None, 1, D_MODEL), sel),
            pl.BlockSpec((None, D_MODEL, QKV_W + SSM_W), sel, pipeline_mode=pl.Buffered(1)),
            pl.BlockSpec((None, 1, ATT_W), sel),
            pl.BlockSpec((None, 1, KV_W), sel),
        ],
        out_specs=[
            pl.BlockSpec((TM_IN, ATT_W), lambda i: (i, 0)),
            pl.BlockSpec((TM_IN, KV_W), lambda i: (i, 0)),
            pl.BlockSpec((TM_IN, KV_W), lambda i: (i, 0)),
            pl.BlockSpec((TM_IN, KV_W), lambda i: (i, 0)),
            pl.BlockSpec((SSM_W, TM_IN // LANES, LANES), lambda i: (0, i, 0)),
        ],
        out_shape=[
            jax.ShapeDtypeStruct((t, ATT_W), BF16),
            jax.ShapeDtypeStruct((t, KV_W), BF16),
            jax.ShapeDtypeStruct((t, KV_W), BF16),
            jax.ShapeDtypeStruct((t, KV_W), BF16),
            jax.ShapeDtypeStruct((SSM_W, t // LANES, LANES), F32),
        ],
        scratch_shapes=[pltpu.VMEM((D_MODEL, QKV_W), BF16), pltpu.VMEM((SSM_W, D_MODEL), BF16)],
        compiler_params=pltpu.CompilerParams(dimension_semantics=("arbitrary",),
                                             vmem_limit_bytes=VMEM_LIMIT),
    )(x, g1, w_in, qg, kg)


def _attn_kernel(sink_ref, q_ref, k_ref, v_ref, vs_ref, bias_ref, o_ref, s_scr, p_scr, *, n_blocks):
    n0 = pl.program_id(1) * ATT_NB
    prev = pl.multiple_of(jnp.maximum(n0 - 1, 0) * WINDOW, WINDOW)
    mid = pl.multiple_of(n0 * WINDOW, WINDOW)
    nxt = pl.multiple_of(jnp.minimum(n0 + ATT_NB, n_blocks - 1) * WINDOW, WINDOW)

    def window(ref):
        return jnp.concatenate([ref[0, pl.ds(prev, WINDOW), :], ref[0, pl.ds(mid, ATT_NB * WINDOW), :],
                                ref[0, pl.ds(nxt, WINDOW), :]], axis=0)

    kw = window(k_ref)
    kj = [kw[:, j * HEAD_DIM:(j + 1) * HEAD_DIM] for j in range(N_KV)]
    vw = window(v_ref)
    vsw = window(vs_ref)
    low = lax.broadcasted_iota(I32, vw.shape, 1) < HEAD_DIM
    one = jnp.ones((), BF16)
    vaug = [[jnp.where(low, vw, one), jnp.where(low, one, vsw)],
            [jnp.where(low, vsw, one), jnp.where(low, one, vw)]]
    low_q = lax.broadcasted_iota(I32, (WINDOW, LANES), 1) < HEAD_DIM

    def scores(b):
        n = n0 + b
        var = jnp.where(n == 0, 0, jnp.where(n == n_blocks - 1, 2, 1))
        qb = q_ref[b * WINDOW:(b + 1) * WINDOW, :]
        for h in range(N_HEADS):
            kh = kj[h // GQA][b * WINDOW:(b + 3) * WINDOW, :]
            s_scr[b % 2, h] = _dot_nt(qb[:, h * HEAD_DIM:(h + 1) * HEAD_DIM], kh) + bias_ref[var, h]

    def softmax(b):
        sink_terms = []
        for h in range(N_HEADS):
            s = s_scr[b % 2, h]
            sk = sink_ref[h] * LOG2E
            m = jnp.maximum(jnp.max(s, axis=-1, keepdims=True), sk)
            p_scr[b % 2, h] = jnp.exp2(s - m).astype(BF16)
            sink_terms.append(jnp.exp2(sk - m))
        return sink_terms

    def weighted(b, sink_terms):
        cols = []
        for c in range(N_HEADS // 2):
            halves = []
            for odd in range(2):
                h = 2 * c + odd
                va = vaug[h // GQA][odd][b * WINDOW:(b + 3) * WINDOW, :]
                res = _dot(p_scr[b % 2, h], va)
                den = pltpu.roll(res, HEAD_DIM, 1) + sink_terms[h]
                halves.append(res * (1.0 / den))
            cols.append(jnp.where(low_q, halves[0], halves[1]))
        o_ref[b * WINDOW:(b + 1) * WINDOW, :] = jnp.concatenate(cols, axis=-1).astype(BF16)

    scores(0)
    for b in range(ATT_NB):
        if b + 1 < ATT_NB:
            scores(b + 1)
        weighted(b, softmax(b))


def _attention(sink, q, k, v, vs, bias, bsz, seq, layer):
    n_blocks = seq // WINDOW
    steps = n_blocks // ATT_NB
    rows = ATT_NB * WINDOW
    kv_spec = pl.BlockSpec((1, seq, KV_W), lambda b, n: (b, 0, 0))
    return pl.pallas_call(
        functools.partial(_attn_kernel, n_blocks=n_blocks),
        name="band_attn",
        grid=(bsz, steps),
        in_specs=[
            pl.BlockSpec(memory_space=pltpu.SMEM),
            pl.BlockSpec((rows, ATT_W), lambda b, n: (b * steps + n, 0)),
            kv_spec, kv_spec, kv_spec,
            pl.BlockSpec((3, N_HEADS, WINDOW, 3 * WINDOW), lambda b, n: (0, 0, 0, 0)),
        ],
        out_specs=pl.BlockSpec((rows, ATT_W), lambda b, n: (b * steps + n, 0)),
        out_shape=jax.ShapeDtypeStruct((bsz * seq, ATT_W), BF16),
        scratch_shapes=[pltpu.VMEM((2, N_HEADS, WINDOW, 3 * WINDOW), F32),
                        pltpu.VMEM((2, N_HEADS, WINDOW, 3 * WINDOW), BF16)],
        compiler_params=pltpu.CompilerParams(dimension_semantics=("arbitrary", "arbitrary"),
                                             vmem_limit_bytes=VMEM_LIMIT),
    )(sink[layer], q, k.reshape(bsz, seq, KV_W), v.reshape(bsz, seq, KV_W),
      vs.reshape(bsz, seq, KV_W), bias)


def _attn_bias(seq):
    q_idx = jnp.arange(WINDOW)[:, None]
    c_idx = jnp.arange(3 * WINDOW)[None, :]
    dist = jnp.abs(q_idx - c_idx + WINDOW)
    slopes = jnp.exp2(-8.0 * jnp.arange(1, N_HEADS + 1, dtype=F32) / N_HEADS)
    alibi = -slopes[:, None, None] * dist.astype(F32)[None]
    in_band = (dist <= WINDOW)[None]
    blk = (c_idx // WINDOW)[None]
    variants = []
    for bad in (0, -1, 2):
        ok = in_band & (blk != bad)
        variants.append(jnp.where(ok, alibi * LOG2E, -1e30))
    return jnp.stack(variants).astype(F32)


def _swap_lane_blocks(vs):
    lane = lax.broadcasted_iota(I32, vs[0].shape, 1)
    lo_half = lane < 2 * SUB
    even_blk = (lane & SUB) == 0
    a0, a1, a2, a3 = vs
    c0 = jnp.where(lo_half, a0, pltpu.roll(a2, 2 * SUB, 1))
    c2 = jnp.where(lo_half, pltpu.roll(a0, 2 * SUB, 1), a2)
    c1 = jnp.where(lo_half, a1, pltpu.roll(a3, 2 * SUB, 1))
    c3 = jnp.where(lo_half, pltpu.roll(a1, 2 * SUB, 1), a3)
    b0 = jnp.where(even_blk, c0, pltpu.roll(c1, SUB, 1))
    b1 = jnp.where(even_blk, pltpu.roll(c0, LANES - SUB, 1), c1)
    b2 = jnp.where(even_blk, c2, pltpu.roll(c3, SUB, 1))
    b3 = jnp.where(even_blk, pltpu.roll(c2, LANES - SUB, 1), c3)
    return [b0, b1, b2, b3]


def _ssm_kernel(u_ref, t_ref, ms_ref, mo_ref, pw_ref, dsk_ref, y_ref, *, chunks_per_seq):
    n_grp = u_ref.shape[0] // SSM_CH
    nbc = u_ref.shape[1]
    is_fwd = lax.broadcasted_iota(I32, (nbc, LANES), 1) < SSM_P
    c_idx = lax.broadcasted_iota(I32, (nbc, LANES), 0) & (chunks_per_seq - 1)

    def cmul(x, a):
        return a[0] * x[0] - a[1] * x[1], a[0] * x[1] + a[1] * x[0]

    def cadd(x, y):
        return x[0] + y[0], x[1] + y[1]

    def shifted(x, dd):
        ok_f = jnp.logical_and(is_fwd, c_idx >= dd)
        ok_b = jnp.logical_and(jnp.logical_not(is_fwd), c_idx < chunks_per_seq - dd)
        return tuple(jnp.where(ok_f, pltpu.roll(t, dd, 0), jnp.where(ok_b, pltpu.roll(t, nbc - dd, 0), 0.0))
                     for t in x)

    def gather(g):
        slabs = [u_ref[g * SSM_CH + h] for h in range(SSM_CH)]
        cols = [_swap_lane_blocks(slabs[4 * q:4 * q + 4]) for q in range(SSM_CH // 4)]
        pieces = [jnp.concatenate([cols[q][i] for q in range(SSM_CH // 4)], axis=1)
                  for i in range(N_SUB)]
        return jnp.concatenate(pieces, axis=0)

    def entering_states(g, s_all):
        def prow(r):
            return pw_ref[g, 0, r:r + 1, :], pw_ref[g, 1, r:r + 1, :]

        def mirror(k):
            lo, hi = k * nbc, (N_SUB - 1 - k) * nbc
            return tuple(jnp.where(is_fwd, s_all[lo:lo + nbc, c:c + LANES], s_all[hi:hi + nbc, c:c + LANES])
                         for c in (0, LANES))

        a32 = prow(0)
        sk = [mirror(k) for k in range(N_SUB)]
        zero = jnp.zeros((nbc, LANES), F32)
        pk = [(zero, zero)]
        for k in range(N_SUB - 1):
            pk.append(cadd(cmul(pk[k], a32), sk[k]))
        x = cadd(cmul(pk[N_SUB - 1], a32), sk[N_SUB - 1])
        for kk in range(_N_SCAN_STEPS):
            x = cadd(x, cmul(shifted(x, 1 << kk), prow(3 + kk)))
        ent = shifted(x, 1)
        qk = [cadd(pk[k], cmul(ent, prow(k - 1)) if k > 0 else ent) for k in range(N_SUB)]
        xin = []
        for i in range(N_SUB):
            j = N_SUB - 1 - i
            xin.append(jnp.concatenate([jnp.where(is_fwd, qk[i][c], qk[j][c]) for c in (0, 1)], axis=1))
        return jnp.concatenate(xin, axis=0).astype(BF16)

    def scatter(g, act):
        for q in range(SSM_CH // 4):
            back = _swap_lane_blocks([act[i * nbc:(i + 1) * nbc, q * LANES:(q + 1) * LANES]
                                      for i in range(N_SUB)])
            for r in range(4):
                y_ref[g * SSM_CH + 4 * q + r] = back[r]

    grp = range(n_grp)
    ust = [gather(g) for g in grp]
    ub = [u.astype(BF16) for u in ust]
    z = [_dot(ub[g], t_ref[g]) for g in grp]
    s_all = [_dot(ub[g], ms_ref[g]) for g in grp]
    xin = [entering_states(g, s_all[g]) for g in grp]
    for g in grp:
        y = z[g] + _dot(xin[g], mo_ref[g]) + dsk_ref[g] * ust[g]
        scatter(g, jax.nn.gelu(y))


def _ssm(u3, t_all, ms_all, mo_all, pw_all, dsk_all, layer, chunks_per_seq):
    nbc = u3.shape[1]
    base = layer * (SSM_G // SSM_GB)
    sel3 = lambda g: (base + g, 0, 0)
    slab_spec = pl.BlockSpec((SSM_GB * SSM_CH, nbc, LANES), lambda g: (g, 0, 0))
    return pl.pallas_call(
        functools.partial(_ssm_kernel, chunks_per_seq=chunks_per_seq),
        name="ssm_mix",
        grid=(SSM_G // SSM_GB,),
        in_specs=[
            slab_spec,
            pl.BlockSpec((SSM_GB, CW, CW), sel3),
            pl.BlockSpec((SSM_GB, CW, 2 * LANES), sel3),
            pl.BlockSpec((SSM_GB, 2 * LANES, CW), sel3),
            pl.BlockSpec((SSM_GB, 2, _POW_ROWS, LANES), lambda g: (base + g, 0, 0, 0)),
            pl.BlockSpec((SSM_GB, 1, CW), sel3),
        ],
        out_specs=slab_spec,
        out_shape=jax.ShapeDtypeStruct((SSM_W, nbc, LANES), F32),
        compiler_params=pltpu.CompilerParams(dimension_semantics=("arbitrary",),
                                             vmem_limit_bytes=VMEM_LIMIT),
    )(u3, t_all, ms_all, mo_all, pw_all, dsk_all)


def _stream_cast(pairs, layer, stage_ref, sem_ref):
    slots, rows, cols = stage_ref.shape
    tiles = [(src, dst, r, c) for src, dst in pairs
             for r in range(0, dst.shape[0], rows) for c in range(0, dst.shape[1], cols)]

    def copy(i):
        src, _, r, c = tiles[i]
        return pltpu.make_async_copy(src.at[layer, pl.ds(r, rows), pl.ds(c, cols)],
                                     stage_ref.at[i % slots], sem_ref.at[i % slots])

    for i in range(min(slots - 1, len(tiles))):
        copy(i).start()
    for i, (_, dst, r, c) in enumerate(tiles):
        if i + slots - 1 < len(tiles):
            copy(i + slots - 1).start()
        copy(i).wait()
        dst[r:r + rows, c:c + cols] = stage_ref[i % slots].astype(BF16)


def _outffn_kernel(x_ref, att_ref, y_ref, g2_ref, wglu_hbm, wout_hbm, w1_hbm, w2_hbm, o_ref,
                   wglu_ref, wout_ref, w1_ref, w2_ref, stage_ref, stage_sem, *, layer):
    @pl.when(pl.program_id(0) == 0)
    def _():
        _stream_cast([(wglu_hbm, wglu_ref), (wout_hbm, wout_ref), (w1_hbm, w1_ref), (w2_hbm, w2_ref)],
                     layer, stage_ref, stage_sem)

    first = (pl.program_id(0) % (Y_CHUNKS * LANES // TM_FFN)) * (TM_FFN // LANES)
    y2d = y_ref.reshape(SSM_W * Y_CHUNKS, LANES)
    parts = [slice(r * FFN_PART, (r + 1) * FFN_PART) for r in range(TM_FFN // FFN_PART)]
    per = FFN_PART // LANES
    y = [jnp.concatenate([y2d[pl.ds(first + r * per + j, SSM_W, stride=Y_CHUNKS), :].T
                          for j in range(per)], axis=0).astype(BF16)
         for r in range(len(parts))]
    gl = [_dot(yy, wglu_ref[...]) for yy in y]
    ssm = [(g[:, :SSM_W] * jax.nn.sigmoid(g[:, SSM_W:])).astype(BF16) for g in gl]
    x = [x_ref[rows, :] + _dot(att_ref[rows, :], wout_ref[:ATT_W, :]) + _dot(s, wout_ref[ATT_W:, :])
         for rows, s in zip(parts, ssm)]
    hn = []
    for xx in x:
        ms = jnp.mean(xx * xx, axis=-1, keepdims=True)
        hn.append((xx * lax.rsqrt(ms + EPS) * g2_ref[...]).astype(BF16))
    hid = [jnp.maximum(_dot(h, w1_ref[...]), 0.0) for h in hn]
    for rows, xx, hh in zip(parts, x, hid):
        o_ref[rows, :] = xx + _dot((hh * hh).astype(BF16), w2_ref[...])


def _outffn(x, att, y3, g2, w_glu, w_out, w_ff1, w_ff2, layer):
    t = x.shape[0]
    hbm = pl.BlockSpec(memory_space=pl.ANY)
    return pl.pallas_call(
        functools.partial(_outffn_kernel, layer=layer),
        name="out_ffn",
        grid=(t // TM_FFN,),
        in_specs=[
            pl.BlockSpec((TM_FFN, D_MODEL), lambda i: (i, 0)),
            pl.BlockSpec((TM_FFN, ATT_W), lambda i: (i, 0)),
            pl.BlockSpec((SSM_W, Y_CHUNKS, LANES), lambda i: (0, i * TM_FFN // (Y_CHUNKS * LANES), 0)),
            pl.BlockSpec((None, 1, D_MODEL), lambda i: (layer, 0, 0)),
            hbm, hbm, hbm, hbm,
        ],
        out_specs=pl.BlockSpec((TM_FFN, D_MODEL), lambda i: (i, 0)),
        out_shape=jax.ShapeDtypeStruct((t, D_MODEL), F32),
        scratch_shapes=[
            pltpu.VMEM((SSM_W, 2 * SSM_W), BF16),
            pltpu.VMEM((D_MODEL, D_MODEL), BF16),
            pltpu.VMEM((D_MODEL, D_FF), BF16),
            pltpu.VMEM((D_FF, D_MODEL), BF16),
            pltpu.VMEM((STAGE_SLOTS, STAGE_ROWS, D_MODEL), F32),
            pltpu.SemaphoreType.DMA((STAGE_SLOTS,)),
        ],
        compiler_params=pltpu.CompilerParams(dimension_semantics=("arbitrary",),
                                             vmem_limit_bytes=VMEM_LIMIT),
    )(x, att, y3, g2, w_glu, w_out, w_ff1, w_ff2)


def _prep_operands(lam_re, lam_im, log_dt, b_re, b_im, c_re, c_im, d_skip):
    depth = lam_re.shape[0]
    lg = depth * SSM_G
    both = lambda a: jnp.concatenate([a[:, 0], a[:, 1]], axis=-1)
    ldt = jnp.broadcast_to(log_dt[..., None], lam_re.shape)
    lam = jnp.stack([both(lam_re), both(lam_im), both(ldt)], axis=2).reshape(lg, 3, LANES)
    bt = jnp.stack([b_re, b_im], axis=2).transpose(0, 1, 2, 4, 3)
    bt = jnp.concatenate([bt, bt], axis=-1).reshape(lg, 2, SSM_CH, LANES)
    cc = jnp.stack([both(c_re), both(c_im)], axis=2).reshape(lg, 2, SSM_CH, LANES)
    dsk = jnp.repeat(d_skip.reshape(lg, 1, SSM_CH), SUB, axis=-1)
    return list(_ssm_prep(lam, bt, cc)) + [dsk]


def kernel(x, norm1, w_in, q_gain, k_gain, sink, lam_re, lam_im, log_dt, b_re, b_im, c_re, c_im,
           d_skip, w_glu, w_out, norm2, w_ff1, w_ff2):
    bsz, seq, _ = x.shape
    depth = norm1.shape[0]
    tokens = bsz * seq
    chunks_per_seq = seq // LANES
    assert seq % (ATT_NB * WINDOW) == 0 and tokens % TM_IN == 0 and tokens % TM_FFN == 0
    assert TM_IN == Y_CHUNKS * LANES and (Y_CHUNKS * LANES) % TM_FFN == 0
    assert chunks_per_seq == 1 << _N_SCAN_STEPS and max(_POW_N) == LANES * chunks_per_seq // 2

    t_all, ms_all, mo_all, pw_all, dsk_all = _prep_operands(lam_re, lam_im, log_dt, b_re, b_im,
                                                           c_re, c_im, d_skip)
    g1 = norm1[:, None, :]
    g2 = norm2[:, None, :]
    qg = (jnp.tile(q_gain, (1, N_HEADS)) * (LOG2E / math.sqrt(HEAD_DIM)))[:, None, :]
    kg = jnp.tile(k_gain, (1, N_KV))[:, None, :]
    bias = _attn_bias(seq)

    xt = x.reshape(tokens, D_MODEL)
    for layer in range(depth):
        q, k, v, vs, u3 = _inproj(xt, g1, w_in, qg, kg, layer)
        att = _attention(sink, q, k, v, vs, bias, bsz, seq, layer)
        y3 = _ssm(u3, t_all, ms_all, mo_all, pw_all, dsk_all, layer, chunks_per_seq)
        xt = _outffn(xt, att, y3, g2, w_glu, w_out, w_ff1, w_ff2, layer)
    return xt.reshape(bsz, seq, D_MODEL)
```

```python
import functools
import math

import jax
import jax.numpy as jnp
from jax import lax
from jax.experimental import pallas as pl
from jax.experimental.pallas import tpu as pltpu

D_MODEL = 1024
N_HEADS = 8
N_KV = 2
GQA = N_HEADS // N_KV
HEAD_DIM = 64
ATT_W = N_HEADS * HEAD_DIM
KV_W = N_KV * HEAD_DIM
QKV_W = ATT_W + 2 * KV_W
WINDOW = 128
SSM_W = 512
SSM_CH = 16
SSM_G = SSM_W // SSM_CH
SSM_P = 64
D_FF = 4 * D_MODEL
EPS = 1e-6
LOG2E = math.log2(math.e)

LANES = 128
SUB = 32
N_SUB = LANES // SUB
CW = SSM_CH * SUB
VMEM_LIMIT = 56 * 1024 * 1024
STAGE_ROWS = 256
STAGE_SLOTS = 8

TM_IN = 1024
TM_HALF = 256
TM_FFN = 512
FFN_PART = 256
Y_CHUNKS = 8
ATT_NB = 8
SSM_GB = 4
PREP_GB = 4

_POW_N = (32, 64, 96, 128, 256, 512, 1024, 2048, 4096)
_POW_ROWS = 16
_N_SCAN_STEPS = 6

F32 = jnp.float32
BF16 = jnp.bfloat16
I32 = jnp.int32


def _dot(a, b):
    return jnp.dot(a, b, preferred_element_type=F32)


def _dot_nt(a, b, precision=None):
    return lax.dot_general(a, b, (((1,), (1,)), ((), ())), precision=precision,
                           preferred_element_type=F32)


def _prep_kernel(lam_ref, bt_ref, c_ref, t_ref, ms_ref, mo_ref, pw_ref,
                 g1r_ref, g1i_ref, mtr_ref, mti_ref):
    grp = range(lam_ref.shape[0])
    s_idx = lax.broadcasted_iota(I32, (SUB, LANES), 0)
    is_fwd = lax.broadcasted_iota(I32, (SUB, LANES), 1) < SSM_P
    fwd16 = lax.broadcasted_iota(I32, (SSM_CH, LANES), 1) < SSM_P
    e_t = jnp.where(is_fwd, s_idx, SUB - 1 - s_idx)

    def cmul(x, y):
        return x[0] * y[0] - x[1] * y[1], x[0] * y[1] + x[1] * y[0]

    def discretise(g):
        lr = lam_ref[g, 0:1, :]
        li = lam_ref[g, 1:2, :]
        dt = jnp.exp(lam_ref[g, 2:3, :])
        mag = jnp.exp(lr * dt)
        abr = mag * jnp.cos(li * dt)
        abi = mag * jnp.sin(li * dt)
        den = lr * lr + li * li
        zr = ((abr - 1.0) * lr + abi * li) / den
        zi = (abi * lr - (abr - 1.0) * li) / den
        btr = bt_ref[g, 0]
        bti = bt_ref[g, 1]
        sq = [(abr, abi)]
        for _ in range(max(_POW_N).bit_length() - 1):
            sq.append(cmul(sq[-1], sq[-1]))
        return sq, zr * btr - zi * bti, zr * bti + zi * btr

    def int_power(sq, e):
        acc = None
        for bit in range(SUB.bit_length() - 1):
            on = (e & (1 << bit)) != 0
            fac = (jnp.where(on, sq[bit][0], 1.0), jnp.where(on, sq[bit][1], 0.0))
            acc = fac if acc is None else cmul(acc, fac)
        return acc

    disc = [discretise(g) for g in grp]
    tpow = [int_power(disc[g][0], e_t) for g in grp]
    mspow = [int_power(disc[g][0], SUB - 1 - e_t) for g in grp]
    mopow = [cmul(tpow[g], disc[g][0][0]) for g in grp]

    for g in grp:
        _, bbr, bbi = disc[g]
        cr = c_ref[g, 0]
        ci = c_ref[g, 1]
        (tr, ti), (msr, msi), (mor, moi) = tpow[g], mspow[g], mopow[g]
        for h in range(SSM_CH):
            rows = slice(h * SUB, (h + 1) * SUB)
            b_r = bbr[h:h + 1, :]
            b_i = bbi[h:h + 1, :]
            ms_ref[g, rows, :LANES] = (msr * b_r - msi * b_i).astype(BF16)
            ms_ref[g, rows, LANES:] = (msr * b_i + msi * b_r).astype(BF16)
            c_r = cr[h:h + 1, :]
            c_i = ci[h:h + 1, :]
            g1r_ref[g, rows, :] = tr * c_r - ti * c_i
            g1i_ref[g, rows, :] = tr * c_i + ti * c_r
            mtr_ref[g, rows, :] = mor * c_r - moi * c_i
            mti_ref[g, rows, :] = -(mor * c_i + moi * c_r)
    for g in grp:
        mo_ref[g, :LANES, :] = mtr_ref[g].T.astype(BF16)
        mo_ref[g, LANES:, :] = mti_ref[g].T.astype(BF16)

    def split(t):
        hi = t.astype(BF16)
        return hi, (t - hi.astype(F32)).astype(BF16)

    def dot_nt_x3(a, b):
        return _dot_nt(a[0], b[0]) + (_dot_nt(a[0], b[1]) + _dot_nt(a[1], b[0]))

    kts = []
    for g in grp:
        _, bbr, bbi = disc[g]
        g1r = split(g1r_ref[g])
        g1i = split(g1i_ref[g])
        kts.append([dot_nt_x3(split(jnp.where(sel, bbr, 0.0)), g1r)
                    - dot_nt_x3(split(jnp.where(sel, bbi, 0.0)), g1i)
                    for sel in (fwd16, jnp.logical_not(fwd16))])

    for g in grp:
        sq = disc[g][0]
        pw_ref[g] = jnp.zeros((2, _POW_ROWS, LANES), F32)
        for r, n in enumerate(_POW_N):
            acc = None
            for bit in range(n.bit_length()):
                if n >> bit & 1:
                    acc = sq[bit] if acc is None else cmul(acc, sq[bit])
            pw_ref[g, 0, r:r + 1, :] = acc[0]
            pw_ref[g, 1, r:r + 1, :] = acc[1]

    s_lane = lax.broadcasted_iota(I32, (SUB, CW), 1) & (SUB - 1)
    s_row = lax.broadcasted_iota(I32, (SUB, CW), 0)
    fwd_mask = s_lane >= s_row
    tau0 = (lax.broadcasted_iota(I32, (SSM_CH, CW), 1) & (SUB - 1)) == 0
    for g in grp:
        kfd = kts[g][0] + jnp.where(tau0, pltpu.roll(kts[g][1], CW - (SUB - 1), 1), 0.0)
        for h in range(SSM_CH):
            kf = jnp.broadcast_to(kfd[h:h + 1, :], (SUB, CW))
            kb = jnp.broadcast_to(kts[g][1][h:h + 1, :], (SUB, CW))
            rf = pltpu.roll(kf, 0, 1, stride=1, stride_axis=0)
            rb = pltpu.roll(kb, CW - (SUB - 1), 1, stride=1, stride_axis=0)
            t_ref[g, h * SUB:(h + 1) * SUB, :] = jnp.where(fwd_mask, rf, rb).astype(BF16)


def _ssm_prep(lam, bt, c):
    lg = lam.shape[0]
    sel3 = lambda i: (i, 0, 0)
    sel4 = lambda i: (i, 0, 0, 0)
    return pl.pallas_call(
        _prep_kernel,
        name="ssm_prep",
        grid=(lg // PREP_GB,),
        in_specs=[
            pl.BlockSpec((PREP_GB, 3, LANES), sel3),
            pl.BlockSpec((PREP_GB, 2, SSM_CH, LANES), sel4),
            pl.BlockSpec((PREP_GB, 2, SSM_CH, LANES), sel4),
        ],
        out_specs=[
            pl.BlockSpec((PREP_GB, CW, CW), sel3),
            pl.BlockSpec((PREP_GB, CW, 2 * LANES), sel3),
            pl.BlockSpec((PREP_GB, 2 * LANES, CW), sel3),
            pl.BlockSpec((PREP_GB, 2, _POW_ROWS, LANES), sel4),
        ],
        out_shape=[
            jax.ShapeDtypeStruct((lg, CW, CW), BF16),
            jax.ShapeDtypeStruct((lg, CW, 2 * LANES), BF16),
            jax.ShapeDtypeStruct((lg, 2 * LANES, CW), BF16),
            jax.ShapeDtypeStruct((lg, 2, _POW_ROWS, LANES), F32),
        ],
        scratch_shapes=[pltpu.VMEM((PREP_GB, CW, LANES), F32)] * 4,
        compiler_params=pltpu.CompilerParams(dimension_semantics=("arbitrary",)),
    )(lam, bt, c)


def _inproj_kernel(x_ref, g1_ref, win_ref, qg_ref, kg_ref,
                   q_ref, k_ref, v_ref, vs_ref, u_ref, wqkv_ref, wut_ref):
    @pl.when(pl.program_id(0) == 0)
    def _():
        wqkv_ref[...] = win_ref[:, :QKV_W].astype(BF16)
        wut_ref[...] = win_ref[:, QKV_W:].T.astype(BF16)

    def head_norm(t, gain):
        low = lax.broadcasted_iota(I32, (t.shape[0], LANES), 1) < HEAD_DIM
        cols = []
        for c in range(t.shape[1] // LANES):
            t2 = t[:, c * LANES:(c + 1) * LANES]
            t2 = t2 * t2
            lo = jnp.sum(jnp.where(low, t2, 0.0), axis=-1, keepdims=True)
            hi = jnp.sum(jnp.where(low, 0.0, t2), axis=-1, keepdims=True)
            cols.append(jnp.where(low, lo, hi))
        m = jnp.concatenate(cols, axis=1) * (1.0 / HEAD_DIM)
        return t * lax.rsqrt(m + EPS) * gain

    halves = [slice(r * TM_HALF, (r + 1) * TM_HALF) for r in range(TM_IN // TM_HALF)]
    hn = []
    for rows in halves:
        x = x_ref[rows, :]
        ms = jnp.mean(x * x, axis=-1, keepdims=True)
        hn.append((x * lax.rsqrt(ms + EPS) * g1_ref[...]).astype(BF16))
    z = [_dot(h, wqkv_ref[...]) for h in hn]
    ut = [_dot_nt(wut_ref[...], h) for h in hn]
    for rows, zz in zip(halves, z):
        q_ref[rows, :] = head_norm(zz[:, :ATT_W], qg_ref[...]).astype(BF16)
        k_ref[rows, :] = head_norm(zz[:, ATT_W:ATT_W + KV_W], kg_ref[...]).astype(BF16)
        vv = zz[:, ATT_W + KV_W:]
        v_ref[rows, :] = vv.astype(BF16)
        vs_ref[rows, :] = pltpu.roll(vv, HEAD_DIM, 1).astype(BF16)
    chunks = TM_IN // LANES
    u2d = u_ref.reshape(SSM_W * chunks, LANES)
    for r, utr in enumerate(ut):
        for jj in range(TM_HALF // LANES):
            j = r * (TM_HALF // LANES) + jj
            u2d[pl.ds(j, SSM_W, stride=chunks), :] = utr[:, jj * LANES:(jj + 1) * LANES]


def _inproj(x, g1, w_in, qg, kg, layer):
    t = x.shape[0]
    sel = lambda i: (layer, 0, 0)
    return pl.pallas_call(
        _inproj_kernel,
        name="inproj",
        grid=(t // TM_IN,),
        in_specs=[
            pl.BlockSpec((TM_IN, D_MODEL), lambda i: (i, 0)),
            pl.BlockSpec((None, 1, D_MODEL), sel),
            pl.BlockSpec((None, D_MODEL, QKV_W + SSM_W), sel, pipeline_mode=pl.Buffered(1)),
            pl.BlockSpec((None, 1, ATT_W), sel),
            pl.BlockSpec((None, 1, KV_W), sel),
        ],
        out_specs=[
            pl.BlockSpec((TM_IN, ATT_W), lambda i: (i, 0)),
            pl.BlockSpec((TM_IN, KV_W), lambda i: (i, 0)),
            pl.BlockSpec((TM_IN, KV_W), lambda i: (i, 0)),
            pl.BlockSpec((TM_IN, KV_W), lambda i: (i, 0)),
            pl.BlockSpec((SSM_W, TM_IN // LANES, LANES), lambda i: (0, i, 0)),
        ],
        out_shape=[
            jax.ShapeDtypeStruct((t, ATT_W), BF16),
            jax.ShapeDtypeStruct((t, KV_W), BF16),
            jax.ShapeDtypeStruct((t, KV_W), BF16),
            jax.ShapeDtypeStruct((t, KV_W), BF16),
            jax.ShapeDtypeStruct((SSM_W, t // LANES, LANES), F32),
        ],
        scratch_shapes=[pltpu.VMEM((D_MODEL, QKV_W), BF16), pltpu.VMEM((SSM_W, D_MODEL), BF16)],
        compiler_params=pltpu.CompilerParams(dimension_semantics=("arbitrary",),
                                             vmem_limit_bytes=VMEM_LIMIT),
    )(x, g1, w_in, qg, kg)


def _attn_kernel(sink_ref, q_ref, k_ref, v_ref, vs_ref, bias_ref, o_ref, s_scr, p_scr, *, n_blocks):
    n0 = pl.program_id(1) * ATT_NB
    prev = pl.multiple_of(jnp.maximum(n0 - 1, 0) * WINDOW, WINDOW)
    mid = pl.multiple_of(n0 * WINDOW, WINDOW)
    nxt = pl.multiple_of(jnp.minimum(n0 + ATT_NB, n_blocks - 1) * WINDOW, WINDOW)

    def window(ref):
        return jnp.concatenate([ref[0, pl.ds(prev, WINDOW), :], ref[0, pl.ds(mid, ATT_NB * WINDOW), :],
                                ref[0, pl.ds(nxt, WINDOW), :]], axis=0)

    kw = window(k_ref)
    kj = [kw[:, j * HEAD_DIM:(j + 1) * HEAD_DIM] for j in range(N_KV)]
    vw = window(v_ref)
    vsw = window(vs_ref)
    low = lax.broadcasted_iota(I32, vw.shape, 1) < HEAD_DIM
    one = jnp.ones((), BF16)
    vaug = [[jnp.where(low, vw, one), jnp.where(low, one, vsw)],
            [jnp.where(low, vsw, one), jnp.where(low, one, vw)]]
    low_q = lax.broadcasted_iota(I32, (WINDOW, LANES), 1) < HEAD_DIM

    def scores(b):
        n = n0 + b
        var = jnp.where(n == 0, 0, jnp.where(n == n_blocks - 1, 2, 1))
        qb = q_ref[b * WINDOW:(b + 1) * WINDOW, :]
        for h in range(N_HEADS):
            kh = kj[h // GQA][b * WINDOW:(b + 3) * WINDOW, :]
            s_scr[b % 2, h] = _dot_nt(qb[:, h * HEAD_DIM:(h + 1) * HEAD_DIM], kh) + bias_ref[var, h]

    def softmax(b):
        sink_terms = []
        for h in range(N_HEADS):
            s = s_scr[b % 2, h]
            sk = sink_ref[h] * LOG2E
            m = jnp.maximum(jnp.max(s, axis=-1, keepdims=True), sk)
            p_scr[b % 2, h] = jnp.exp2(s - m).astype(BF16)
            sink_terms.append(jnp.exp2(sk - m))
        return sink_terms

    def weighted(b, sink_terms):
        cols = []
        for c in range(N_HEADS // 2):
            halves = []
            for odd in range(2):
                h = 2 * c + odd
                va = vaug[h // GQA][odd][b * WINDOW:(b + 3) * WINDOW, :]
                res = _dot(p_scr[b % 2, h], va)
                den = pltpu.roll(res, HEAD_DIM, 1) + sink_terms[h]
                halves.append(res * (1.0 / den))
            cols.append(jnp.where(low_q, halves[0], halves[1]))
        o_ref[b * WINDOW:(b + 1) * WINDOW, :] = jnp.concatenate(cols, axis=-1).astype(BF16)

    scores(0)
    for b in range(ATT_NB):
        if b + 1 < ATT_NB:
            scores(b + 1)
        weighted(b, softmax(b))


def _attention(sink, q, k, v, vs, bias, bsz, seq, layer):
    n_blocks = seq // WINDOW
    steps = n_blocks // ATT_NB
    rows = ATT_NB * WINDOW
    kv_spec = pl.BlockSpec((1, seq, KV_W), lambda b, n: (b, 0, 0))
    return pl.pallas_call(
        functools.partial(_attn_kernel, n_blocks=n_blocks),
        name="band_attn",
        grid=(bsz, steps),
        in_specs=[
            pl.BlockSpec(memory_space=pltpu.SMEM),
            pl.BlockSpec((rows, ATT_W), lambda b, n: (b * steps + n, 0)),
            kv_spec, kv_spec, kv_spec,
            pl.BlockSpec((3, N_HEADS, WINDOW, 3 * WINDOW), lambda b, n: (0, 0, 0, 0)),
        ],
        out_specs=pl.BlockSpec((rows, ATT_W), lambda b, n: (b * steps + n, 0)),
        out_shape=jax.ShapeDtypeStruct((bsz * seq, ATT_W), BF16),
        scratch_shapes=[pltpu.VMEM((2, N_HEADS, WINDOW, 3 * WINDOW), F32),
                        pltpu.VMEM((2, N_HEADS, WINDOW, 3 * WINDOW), BF16)],
        compiler_params=pltpu.CompilerParams(dimension_semantics=("arbitrary", "arbitrary"),
                                             vmem_limit_bytes=VMEM_LIMIT),
    )(sink[layer], q, k.reshape(bsz, seq, KV_W), v.reshape(bsz, seq, KV_W),
      vs.reshape(bsz, seq, KV_W), bias)


def _attn_bias(seq):
    q_idx = jnp.arange(WINDOW)[:, None]
    c_idx = jnp.arange(3 * WINDOW)[None, :]
    dist = jnp.abs(q_idx - c_idx + WINDOW)
    slopes = jnp.exp2(-8.0 * jnp.arange(1, N_HEADS + 1, dtype=F32) / N_HEADS)
    alibi = -slopes[:, None, None] * dist.astype(F32)[None]
    in_band = (dist <= WINDOW)[None]
    blk = (c_idx // WINDOW)[None]
    variants = []
    for bad in (0, -1, 2):
        ok = in_band & (blk != bad)
        variants.append(jnp.where(ok, alibi * LOG2E, -1e30))
    return jnp.stack(variants).astype(F32)


def _swap_lane_blocks(vs):
    lane = lax.broadcasted_iota(I32, vs[0].shape, 1)
    lo_half = lane < 2 * SUB
    even_blk = (lane & SUB) == 0
    a0, a1, a2, a3 = vs
    c0 = jnp.where(lo_half, a0, pltpu.roll(a2, 2 * SUB, 1))
    c2 = jnp.where(lo_half, pltpu.roll(a0, 2 * SUB, 1), a2)
    c1 = jnp.where(lo_half, a1, pltpu.roll(a3, 2 * SUB, 1))
    c3 = jnp.where(lo_half, pltpu.roll(a1, 2 * SUB, 1), a3)
    b0 = jnp.where(even_blk, c0, pltpu.roll(c1, SUB, 1))
    b1 = jnp.where(even_blk, pltpu.roll(c0, LANES - SUB, 1), c1)
    b2 = jnp.where(even_blk, c2, pltpu.roll(c3, SUB, 1))
    b3 = jnp.where(even_blk, pltpu.roll(c2, LANES - SUB, 1), c3)
    return [b0, b1, b2, b3]


def _ssm_kernel(u_ref, t_ref, ms_ref, mo_ref, pw_ref, dsk_ref, y_ref, *, chunks_per_seq):
    n_grp = u_ref.shape[0] // SSM_CH
    nbc = u_ref.shape[1]
    is_fwd = lax.broadcasted_iota(I32, (nbc, LANES), 1) < SSM_P
    c_idx = lax.broadcasted_iota(I32, (nbc, LANES), 0) & (chunks_per_seq - 1)

    def cmul(x, a):
        return a[0] * x[0] - a[1] * x[1], a[0] * x[1] + a[1] * x[0]

    def cadd(x, y):
        return x[0] + y[0], x[1] + y[1]

    def shifted(x, dd):
        ok_f = jnp.logical_and(is_fwd, c_idx >= dd)
        ok_b = jnp.logical_and(jnp.logical_not(is_fwd), c_idx < chunks_per_seq - dd)
        return tuple(jnp.where(ok_f, pltpu.roll(t, dd, 0), jnp.where(ok_b, pltpu.roll(t, nbc - dd, 0), 0.0))
                     for t in x)

    def gather(g):
        slabs = [u_ref[g * SSM_CH + h] for h in range(SSM_CH)]
        cols = [_swap_lane_blocks(slabs[4 * q:4 * q + 4]) for q in range(SSM_CH // 4)]
        pieces = [jnp.concatenate([cols[q][i] for q in range(SSM_CH // 4)], axis=1)
                  for i in range(N_SUB)]
        return jnp.concatenate(pieces, axis=0)

    def entering_states(g, s_all):
        def prow(r):
            return pw_ref[g, 0, r:r + 1, :], pw_ref[g, 1, r:r + 1, :]

        def mirror(k):
            lo, hi = k * nbc, (N_SUB - 1 - k) * nbc
            return tuple(jnp.where(is_fwd, s_all[lo:lo + nbc, c:c + LANES], s_all[hi:hi + nbc, c:c + LANES])
                         for c in (0, LANES))

        a32 = prow(0)
        sk = [mirror(k) for k in range(N_SUB)]
        zero = jnp.zeros((nbc, LANES), F32)
        pk = [(zero, zero)]
        for k in range(N_SUB - 1):
            pk.append(cadd(cmul(pk[k], a32), sk[k]))
        x = cadd(cmul(pk[N_SUB - 1], a32), sk[N_SUB - 1])
        for kk in range(_N_SCAN_STEPS):
            x = cadd(x, cmul(shifted(x, 1 << kk), prow(3 + kk)))
        ent = shifted(x, 1)
        qk = [cadd(pk[k], cmul(ent, prow(k - 1)) if k > 0 else ent) for k in range(N_SUB)]
        xin = []
        for i in range(N_SUB):
            j = N_SUB - 1 - i
            xin.append(jnp.concatenate([jnp.where(is_fwd, qk[i][c], qk[j][c]) for c in (0, 1)], axis=1))
        return jnp.concatenate(xin, axis=0).astype(BF16)

    def scatter(g, act):
        for q in range(SSM_CH // 4):
            back = _swap_lane_blocks([act[i * nbc:(i + 1) * nbc, q * LANES:(q + 1) * LANES]
                                      for i in range(N_SUB)])
            for r in range(4):
                y_ref[g * SSM_CH + 4 * q + r] = back[r]

    grp = range(n_grp)
    ust = [gather(g) for g in grp]
    ub = [u.astype(BF16) for u in ust]
    z = [_dot(ub[g], t_ref[g]) for g in grp]
    s_all = [_dot(ub[g], ms_ref[g]) for g in grp]
    xin = [entering_states(g, s_all[g]) for g in grp]
    for g in grp:
        y = z[g] + _dot(xin[g], mo_ref[g]) + dsk_ref[g] * ust[g]
        scatter(g, jax.nn.gelu(y))


def _ssm(u3, t_all, ms_all, mo_all, pw_all, dsk_all, layer, chunks_per_seq):
    nbc = u3.shape[1]
    base = layer * (SSM_G // SSM_GB)
    sel3 = lambda g: (base + g, 0, 0)
    slab_spec = pl.BlockSpec((SSM_GB * SSM_CH, nbc, LANES), lambda g: (g, 0, 0))
    return pl.pallas_call(
        functools.partial(_ssm_kernel, chunks_per_seq=chunks_per_seq),
        name="ssm_mix",
        grid=(SSM_G // SSM_GB,),
        in_specs=[
            slab_spec,
            pl.BlockSpec((SSM_GB, CW, CW), sel3),
            pl.BlockSpec((SSM_GB, CW, 2 * LANES), sel3),
            pl.BlockSpec((SSM_GB, 2 * LANES, CW), sel3),
            pl.BlockSpec((SSM_GB, 2, _POW_ROWS, LANES), lambda g: (base + g, 0, 0, 0)),
            pl.BlockSpec((SSM_GB, 1, CW), sel3),
        ],
        out_specs=slab_spec,
        out_shape=jax.ShapeDtypeStruct((SSM_W, nbc, LANES), F32),
        compiler_params=pltpu.CompilerParams(dimension_semantics=("arbitrary",),
                                             vmem_limit_bytes=VMEM_LIMIT),
    )(u3, t_all, ms_all, mo_all, pw_all, dsk_all)


def _stream_cast(pairs, layer, stage_ref, sem_ref):
    slots, rows, cols = stage_ref.shape
    tiles = [(src, dst, r, c) for src, dst in pairs
             for r in range(0, dst.shape[0], rows) for c in range(0, dst.shape[1], cols)]

    def copy(i):
        src, _, r, c = tiles[i]
        return pltpu.make_async_copy(src.at[layer, pl.ds(r, rows), pl.ds(c, cols)],
                                     stage_ref.at[i % slots], sem_ref.at[i % slots])

    for i in range(min(slots - 1, len(tiles))):
        copy(i).start()
    for i, (_, dst, r, c) in enumerate(tiles):
        if i + slots - 1 < len(tiles):
            copy(i + slots - 1).start()
        copy(i).wait()
        dst[r:r + rows, c:c + cols] = stage_ref[i % slots].astype(BF16)


def _outffn_kernel(x_ref, att_ref, y_ref, g2_ref, wglu_hbm, wout_hbm, w1_hbm, w2_hbm, o_ref,
                   wglu_ref, wout_ref, w1_ref, w2_ref, stage_ref, stage_sem, *, layer):
    @pl.when(pl.program_id(0) == 0)
    def _():
        _stream_cast([(wglu_hbm, wglu_ref), (wout_hbm, wout_ref), (w1_hbm, w1_ref), (w2_hbm, w2_ref)],
                     layer, stage_ref, stage_sem)

    first = (pl.program_id(0) % (Y_CHUNKS * LANES // TM_FFN)) * (TM_FFN // LANES)
    y2d = y_ref.reshape(SSM_W * Y_CHUNKS, LANES)
    parts = [slice(r * FFN_PART, (r + 1) * FFN_PART) for r in range(TM_FFN // FFN_PART)]
    per = FFN_PART // LANES
    y = [jnp.concatenate([y2d[pl.ds(first + r * per + j, SSM_W, stride=Y_CHUNKS), :].T
                          for j in range(per)], axis=0).astype(BF16)
         for r in range(len(parts))]
    gl = [_dot(yy, wglu_ref[...]) for yy in y]
    ssm = [(g[:, :SSM_W] * jax.nn.sigmoid(g[:, SSM_W:])).astype(BF16) for g in gl]
    x = [x_ref[rows, :] + _dot(att_ref[rows, :], wout_ref[:ATT_W, :]) + _dot(s, wout_ref[ATT_W:, :])
         for rows, s in zip(parts, ssm)]
    hn = []
    for xx in x:
        ms = jnp.mean(xx * xx, axis=-1, keepdims=True)
        hn.append((xx * lax.rsqrt(ms + EPS) * g2_ref[...]).astype(BF16))
    hid = [jnp.maximum(_dot(h, w1_ref[...]), 0.0) for h in hn]
    for rows, xx, hh in zip(parts, x, hid):
        o_ref[rows, :] = xx + _dot((hh * hh).astype(BF16), w2_ref[...])


def _outffn(x, att, y3, g2, w_glu, w_out, w_ff1, w_ff2, layer):
    t = x.shape[0]
    hbm = pl.BlockSpec(memory_space=pl.ANY)
    return pl.pallas_call(
        functools.partial(_outffn_kernel, layer=layer),
        name="out_ffn",
        grid=(t // TM_FFN,),
        in_specs=[
            pl.BlockSpec((TM_FFN, D_MODEL), lambda i: (i, 0)),
            pl.BlockSpec((TM_FFN, ATT_W), lambda i: (i, 0)),
            pl.BlockSpec((SSM_W, Y_CHUNKS, LANES), lambda i: (0, i * TM_FFN // (Y_CHUNKS * LANES), 0)),
            pl.BlockSpec((None, 1, D_MODEL), lambda i: (layer, 0, 0)),
            hbm, hbm, hbm, hbm,
        ],
        out_specs=pl.BlockSpec((TM_FFN, D_MODEL), lambda i: (i, 0)),
        out_shape=jax.ShapeDtypeStruct((t, D_MODEL), F32),
        scratch_shapes=[
            pltpu.VMEM((SSM_W, 2 * SSM_W), BF16),
            pltpu.VMEM((D_MODEL, D_MODEL), BF16),
            pltpu.VMEM((D_MODEL, D_FF), BF16),
            pltpu.VMEM((D_FF, D_MODEL), BF16),
            pltpu.VMEM((STAGE_SLOTS, STAGE_ROWS, D_MODEL), F32),
            pltpu.SemaphoreType.DMA((STAGE_SLOTS,)),
        ],
        compiler_params=pltpu.CompilerParams(dimension_semantics=("arbitrary",),
                                             vmem_limit_bytes=VMEM_LIMIT),
    )(x, att, y3, g2, w_glu, w_out, w_ff1, w_ff2)


def _prep_operands(lam_re, lam_im, log_dt, b_re, b_im, c_re, c_im, d_skip):
    depth = lam_re.shape[0]
    lg = depth * SSM_G
    both = lambda a: jnp.concatenate([a[:, 0], a[:, 1]], axis=-1)
    ldt = jnp.broadcast_to(log_dt[..., None], lam_re.shape)
    lam = jnp.stack([both(lam_re), both(lam_im), both(ldt)], axis=2).reshape(lg, 3, LANES)
    bt = jnp.stack([b_re, b_im], axis=2).transpose(0, 1, 2, 4, 3)
    bt = jnp.concatenate([bt, bt], axis=-1).reshape(lg, 2, SSM_CH, LANES)
    cc = jnp.stack([both(c_re), both(c_im)], axis=2).reshape(lg, 2, SSM_CH, LANES)
    dsk = jnp.repeat(d_skip.reshape(lg, 1, SSM_CH), SUB, axis=-1)
    return list(_ssm_prep(lam, bt, cc)) + [dsk]


def kernel(x, norm1, w_in, q_gain, k_gain, sink, lam_re, lam_im, log_dt, b_re, b_im, c_re, c_im,
           d_skip, w_glu, w_out, norm2, w_ff1, w_ff2):
    bsz, seq, _ = x.shape
    depth = norm1.shape[0]
    tokens = bsz * seq
    chunks_per_seq = seq // LANES
    assert seq % (ATT_NB * WINDOW) == 0 and tokens % TM_IN == 0 and tokens % TM_FFN == 0
    assert TM_IN == Y_CHUNKS * LANES and (Y_CHUNKS * LANES) % TM_FFN == 0
    assert chunks_per_seq == 1 << _N_SCAN_STEPS and max(_POW_N) == LANES * chunks_per_seq // 2

    t_all, ms_all, mo_all, pw_all, dsk_all = _prep_operands(lam_re, lam_im, log_dt, b_re, b_im,
                                                           c_re, c_im, d_skip)
    g1 = norm1[:, None, :]
    g2 = norm2[:, None, :]
    qg = (jnp.tile(q_gain, (1, N_HEADS)) * (LOG2E / math.sqrt(HEAD_DIM)))[:, None, :]
    kg = jnp.tile(k_gain, (1, N_KV))[:, None, :]
    bias = _attn_bias(seq)

    xt = x.reshape(tokens, D_MODEL)
    for layer in range(depth):
        q, k, v, vs, u3 = _inproj(xt, g1, w_in, qg, kg, layer)
        att = _attention(sink, q, k, v, vs, bias, bsz, seq, layer)
        y3 = _ssm(u3, t_all, ms_all, mo_all, pw_all, dsk_all, layer, chunks_per_seq)
        xt = _outffn(xt, att, y3, g2, w_glu, w_out, w_ff1, w_ff2, layer)
    return xt.reshape(bsz, seq, D_MODEL)
```

```python
import functools
import math

import jax
import jax.numpy as jnp
from jax import lax
from jax.experimental import pallas as pl
from jax.experimental.pallas import tpu as pltpu

D_MODEL = 1024
N_HEADS = 8
N_KV = 2
GQA = N_HEADS // N_KV
HEAD_DIM = 64
ATT_W = N_HEADS * HEAD_DIM
KV_W = N_KV * HEAD_DIM
QKV_W = ATT_W + 2 * KV_W
WINDOW = 128
SSM_W = 512
SSM_CH = 16
SSM_G = SSM_W // SSM_CH
SSM_P = 64
D_FF = 4 * D_MODEL
EPS = 1e-6
LOG2E = math.log2(math.e)

LANES = 128
SUB = 32
N_SUB = LANES // SUB
CW = SSM_CH * SUB
VMEM_LIMIT = 56 * 1024 * 1024
STAGE_ROWS = 256
STAGE_SLOTS = 12

TM_IN = 1024
TM_HALF = 256
TM_FFN = 512
FFN_PART = 256
Y_CHUNKS = 8
ATT_NB = 8
SSM_GB = 4
PREP_GB = 8

_POW_N = (32, 64, 96, 128, 256, 512, 1024, 2048, 4096)
_POW_ROWS = 16
_N_SCAN_STEPS = 6

F32 = jnp.float32
BF16 = jnp.bfloat16
I32 = jnp.int32


def _dot(a, b):
    return jnp.dot(a, b, preferred_element_type=F32)


def _dot_nt(a, b, precision=None):
    return lax.dot_general(a, b, (((1,), (1,)), ((), ())), precision=precision,
                           preferred_element_type=F32)


def _prep_kernel(lam_ref, bt_ref, c_ref, t_ref, ms_ref, mo_ref, pw_ref,
                 g1r_ref, g1i_ref, mtr_ref, mti_ref):
    grp = range(lam_ref.shape[0])
    s_idx = lax.broadcasted_iota(I32, (SUB, LANES), 0)
    is_fwd = lax.broadcasted_iota(I32, (SUB, LANES), 1) < SSM_P
    fwd16 = lax.broadcasted_iota(I32, (SSM_CH, LANES), 1) < SSM_P
    e_t = jnp.where(is_fwd, s_idx, SUB - 1 - s_idx)

    def cmul(x, y):
        return x[0] * y[0] - x[1] * y[1], x[0] * y[1] + x[1] * y[0]

    def discretise(g):
        lr = lam_ref[g, 0:1, :]
        li = lam_ref[g, 1:2, :]
        dt = jnp.exp(lam_ref[g, 2:3, :])
        mag = jnp.exp(lr * dt)
        abr = mag * jnp.cos(li * dt)
        abi = mag * jnp.sin(li * dt)
        den = lr * lr + li * li
        zr = ((abr - 1.0) * lr + abi * li) / den
        zi = (abi * lr - (abr - 1.0) * li) / den
        btr = bt_ref[g, 0]
        bti = bt_ref[g, 1]
        sq = [(abr, abi)]
        for _ in range(max(_POW_N).bit_length() - 1):
            sq.append(cmul(sq[-1], sq[-1]))
        return sq, zr * btr - zi * bti, zr * bti + zi * btr

    def int_power(sq, e):
        acc = None
        for bit in range(SUB.bit_length() - 1):
            on = (e & (1 << bit)) != 0
            fac = (jnp.where(on, sq[bit][0], 1.0), jnp.where(on, sq[bit][1], 0.0))
            acc = fac if acc is None else cmul(acc, fac)
        return acc

    disc = [discretise(g) for g in grp]
    tpow = [int_power(disc[g][0], e_t) for g in grp]
    mspow = [int_power(disc[g][0], SUB - 1 - e_t) for g in grp]
    mopow = [cmul(tpow[g], disc[g][0][0]) for g in grp]

    for g in grp:
        _, bbr, bbi = disc[g]
        cr = c_ref[g, 0]
        ci = c_ref[g, 1]
        (tr, ti), (msr, msi), (mor, moi) = tpow[g], mspow[g], mopow[g]
        for h in range(SSM_CH):
            rows = slice(h * SUB, (h + 1) * SUB)
            b_r = bbr[h:h + 1, :]
            b_i = bbi[h:h + 1, :]
            ms_ref[g, rows, :LANES] = (msr * b_r - msi * b_i).astype(BF16)
            ms_ref[g, rows, LANES:] = (msr * b_i + msi * b_r).astype(BF16)
            c_r = cr[h:h + 1, :]
            c_i = ci[h:h + 1, :]
            g1r_ref[g, rows, :] = tr * c_r - ti * c_i
            g1i_ref[g, rows, :] = tr * c_i + ti * c_r
            mtr_ref[g, rows, :] = mor * c_r - moi * c_i
            mti_ref[g, rows, :] = -(mor * c_i + moi * c_r)
    for g in grp:
        mo_ref[g, :LANES, :] = mtr_ref[g].T.astype(BF16)
        mo_ref[g, LANES:, :] = mti_ref[g].T.astype(BF16)

    def split(t):
        hi = t.astype(BF16)
        return hi, (t - hi.astype(F32)).astype(BF16)

    def dot_nt_x3(a, b):
        return _dot_nt(a[0], b[0]) + (_dot_nt(a[0], b[1]) + _dot_nt(a[1], b[0]))

    kts = []
    for g in grp:
        _, bbr, bbi = disc[g]
        g1r = split(g1r_ref[g])
        g1i = split(g1i_ref[g])
        kts.append([dot_nt_x3(split(jnp.where(sel, bbr, 0.0)), g1r)
                    - dot_nt_x3(split(jnp.where(sel, bbi, 0.0)), g1i)
                    for sel in (fwd16, jnp.logical_not(fwd16))])

    for g in grp:
        sq = disc[g][0]
        pw_ref[g] = jnp.zeros((2, _POW_ROWS, LANES), F32)
        for r, n in enumerate(_POW_N):
            acc = None
            for bit in range(n.bit_length()):
                if n >> bit & 1:
                    acc = sq[bit] if acc is None else cmul(acc, sq[bit])
            pw_ref[g, 0, r:r + 1, :] = acc[0]
            pw_ref[g, 1, r:r + 1, :] = acc[1]

    s_lane = lax.broadcasted_iota(I32, (SUB, CW), 1) & (SUB - 1)
    s_row = lax.broadcasted_iota(I32, (SUB, CW), 0)
    fwd_mask = s_lane >= s_row
    tau0 = (lax.broadcasted_iota(I32, (SSM_CH, CW), 1) & (SUB - 1)) == 0
    for g in grp:
        kfd = kts[g][0] + jnp.where(tau0, pltpu.roll(kts[g][1], CW - (SUB - 1), 1), 0.0)
        for h in range(SSM_CH):
            kf = jnp.broadcast_to(kfd[h:h + 1, :], (SUB, CW))
            kb = jnp.broadcast_to(kts[g][1][h:h + 1, :], (SUB, CW))
            rf = pltpu.roll(kf, 0, 1, stride=1, stride_axis=0)
            rb = pltpu.roll(kb, CW - (SUB - 1), 1, stride=1, stride_axis=0)
            t_ref[g, h * SUB:(h + 1) * SUB, :] = jnp.where(fwd_mask, rf, rb).astype(BF16)


def _ssm_prep(lam, bt, c):
    lg = lam.shape[0]
    sel3 = lambda i: (i, 0, 0)
    sel4 = lambda i: (i, 0, 0, 0)
    return pl.pallas_call(
        _prep_kernel,
        name="ssm_prep",
        grid=(lg // PREP_GB,),
        in_specs=[
            pl.BlockSpec((PREP_GB, 3, LANES), sel3),
            pl.BlockSpec((PREP_GB, 2, SSM_CH, LANES), sel4),
            pl.BlockSpec((PREP_GB, 2, SSM_CH, LANES), sel4),
        ],
        out_specs=[
            pl.BlockSpec((PREP_GB, CW, CW), sel3),
            pl.BlockSpec((PREP_GB, CW, 2 * LANES), sel3),
            pl.BlockSpec((PREP_GB, 2 * LANES, CW), sel3),
            pl.BlockSpec((PREP_GB, 2, _POW_ROWS, LANES), sel4),
        ],
        out_shape=[
            jax.ShapeDtypeStruct((lg, CW, CW), BF16),
            jax.ShapeDtypeStruct((lg, CW, 2 * LANES), BF16),
            jax.ShapeDtypeStruct((lg, 2 * LANES, CW), BF16),
            jax.ShapeDtypeStruct((lg, 2, _POW_ROWS, LANES), F32),
        ],
        scratch_shapes=[pltpu.VMEM((PREP_GB, CW, LANES), F32)] * 4,
        compiler_params=pltpu.CompilerParams(dimension_semantics=("arbitrary",)),
    )(lam, bt, c)


def _inproj_kernel(x_ref, g1_ref, win_ref, qg_ref, kg_ref,
                   q_ref, k_ref, v_ref, vs_ref, u_ref, wqkv_ref, wut_ref):
    @pl.when(pl.program_id(0) == 0)
    def _():
        wqkv_ref[...] = win_ref[:, :QKV_W].astype(BF16)
        wut_ref[...] = win_ref[:, QKV_W:].T.astype(BF16)

    def head_norm(t, gain):
        low = lax.broadcasted_iota(I32, (t.shape[0], LANES), 1) < HEAD_DIM
        cols = []
        for c in range(t.shape[1] // LANES):
            t2 = t[:, c * LANES:(c + 1) * LANES]
            t2 = t2 * t2
            lo = jnp.sum(jnp.where(low, t2, 0.0), axis=-1, keepdims=True)
            hi = jnp.sum(jnp.where(low, 0.0, t2), axis=-1, keepdims=True)
            cols.append(jnp.where(low, lo, hi))
        m = jnp.concatenate(cols, axis=1) * (1.0 / HEAD_DIM)
        return t * lax.rsqrt(m + EPS) * gain

    halves = [slice(r * TM_HALF, (r + 1) * TM_HALF) for r in range(TM_IN // TM_HALF)]
    hn = []
    for rows in halves:
        x = x_ref[rows, :]
        ms = jnp.mean(x * x, axis=-1, keepdims=True)
        hn.append((x * lax.rsqrt(ms + EPS) * g1_ref[...]).astype(BF16))
    z = [_dot(h, wqkv_ref[...]) for h in hn]
    ut = [_dot_nt(wut_ref[...], h) for h in hn]
    for rows, zz in zip(halves, z):
        q_ref[rows, :] = head_norm(zz[:, :ATT_W], qg_ref[...]).astype(BF16)
        k_ref[rows, :] = head_norm(zz[:, ATT_W:ATT_W + KV_W], kg_ref[...]).astype(BF16)
        vv = zz[:, ATT_W + KV_W:]
        v_ref[rows, :] = vv.astype(BF16)
        vs_ref[rows, :] = pltpu.roll(vv, HEAD_DIM, 1).astype(BF16)
    chunks = TM_IN // LANES
    u2d = u_ref.reshape(SSM_W * chunks, LANES)
    for r, utr in enumerate(ut):
        for jj in range(TM_HALF // LANES):
            j = r * (TM_HALF // LANES) + jj
            u2d[pl.ds(j, SSM_W, stride=chunks), :] = utr[:, jj * LANES:(jj + 1) * LANES]


def _inproj(x, g1, w_in, qg, kg, layer):
    t = x.shape[0]
    sel = lambda i: (layer, 0, 0)
    return pl.pallas_call(
        _inproj_kernel,
        name="inproj",
        grid=(t // TM_IN,),
        in_specs=[
            pl.BlockSpec((TM_IN, D_MODEL), lambda i: (i, 0)),
            pl.BlockSpec((None, 1, D_MODEL), sel),
            pl.BlockSpec((None, D_MODEL, QKV_W + SSM_W), sel, pipeline_mode=pl.Buffered(1)),
            pl.BlockSpec((None, 1, ATT_W), sel),
            pl.BlockSpec((None, 1, KV_W), sel),
        ],
        out_specs=[
            pl.BlockSpec((TM_IN, ATT_W), lambda i: (i, 0)),
            pl.BlockSpec((TM_IN, KV_W), lambda i: (i, 0)),
            pl.BlockSpec((TM_IN, KV_W), lambda i: (i, 0)),
            pl.BlockSpec((TM_IN, KV_W), lambda i: (i, 0)),
            pl.BlockSpec((SSM_W, TM_IN // LANES, LANES), lambda i: (0, i, 0)),
        ],
        out_shape=[
            jax.ShapeDtypeStruct((t, ATT_W), BF16),
            jax.ShapeDtypeStruct((t, KV_W), BF16),
            jax.ShapeDtypeStruct((t, KV_W), BF16),
            jax.ShapeDtypeStruct((t, KV_W), BF16),
            jax.ShapeDtypeStruct((SSM_W, t // LANES, LANES), F32),
        ],
        scratch_shapes=[pltpu.VMEM((D_MODEL, QKV_W), BF16), pltpu.VMEM((SSM_W, D_MODEL), BF16)],
        compiler_params=pltpu.CompilerParams(dimension_semantics=("arbitrary",),
                                             vmem_limit_bytes=VMEM_LIMIT),
    )(x, g1, w_in, qg, kg)


def _attn_kernel(sink_ref, q_ref, k_ref, v_ref, vs_ref, bias_ref, o_ref, s_scr, p_scr, *, n_blocks):
    n0 = pl.program_id(1) * ATT_NB
    prev = pl.multiple_of(jnp.maximum(n0 - 1, 0) * WINDOW, WINDOW)
    mid = pl.multiple_of(n0 * WINDOW, WINDOW)
    nxt = pl.multiple_of(jnp.minimum(n0 + ATT_NB, n_blocks - 1) * WINDOW, WINDOW)

    def window(ref):
        return jnp.concatenate([ref[0, pl.ds(prev, WINDOW), :], ref[0, pl.ds(mid, ATT_NB * WINDOW), :],
                                ref[0, pl.ds(nxt, WINDOW), :]], axis=0)

    kw = window(k_ref)
    kj = [kw[:, j * HEAD_DIM:(j + 1) * HEAD_DIM] for j in range(N_KV)]
    vw = window(v_ref)
    vsw = window(vs_ref)
    low = lax.broadcasted_iota(I32, vw.shape, 1) < HEAD_DIM
    one = jnp.ones((), BF16)
    vaug = [[jnp.where(low, vw, one), jnp.where(low, one, vsw)],
            [jnp.where(low, vsw, one), jnp.where(low, one, vw)]]
    low_q = lax.broadcasted_iota(I32, (WINDOW, LANES), 1) < HEAD_DIM

    def scores(b):
        n = n0 + b
        var = jnp.where(n == 0, 0, jnp.where(n == n_blocks - 1, 2, 1))
        qb = q_ref[b * WINDOW:(b + 1) * WINDOW, :]
        for h in range(N_HEADS):
            kh = kj[h // GQA][b * WINDOW:(b + 3) * WINDOW, :]
            s_scr[b % 2, h] = _dot_nt(qb[:, h * HEAD_DIM:(h + 1) * HEAD_DIM], kh) + bias_ref[var, h]

    def softmax(b):
        sink_terms = []
        for h in range(N_HEADS):
            s = s_scr[b % 2, h]
            sk = sink_ref[h] * LOG2E
            m = jnp.maximum(jnp.max(s, axis=-1, keepdims=True), sk)
            p_scr[b % 2, h] = jnp.exp2(s - m).astype(BF16)
            sink_terms.append(jnp.exp2(sk - m))
        return sink_terms

    def weighted(b, sink_terms):
        cols = []
        for c in range(N_HEADS // 2):
            halves = []
            for odd in range(2):
                h = 2 * c + odd
                va = vaug[h // GQA][odd][b * WINDOW:(b + 3) * WINDOW, :]
                res = _dot(p_scr[b % 2, h], va)
                den = pltpu.roll(res, HEAD_DIM, 1) + sink_terms[h]
                halves.append(res * (1.0 / den))
            cols.append(jnp.where(low_q, halves[0], halves[1]))
        o_ref[b * WINDOW:(b + 1) * WINDOW, :] = jnp.concatenate(cols, axis=-1).astype(BF16)

    scores(0)
    for b in range(ATT_NB):
        if b + 1 < ATT_NB:
            scores(b + 1)
        weighted(b, softmax(b))


def _attention(sink, q, k, v, vs, bias, bsz, seq, layer):
    n_blocks = seq // WINDOW
    steps = n_blocks // ATT_NB
    rows = ATT_NB * WINDOW
    kv_spec = pl.BlockSpec((1, seq, KV_W), lambda b, n: (b, 0, 0))
    return pl.pallas_call(
        functools.partial(_attn_kernel, n_blocks=n_blocks),
        name="band_attn",
        grid=(bsz, steps),
        in_specs=[
            pl.BlockSpec(memory_space=pltpu.SMEM),
            pl.BlockSpec((rows, ATT_W), lambda b, n: (b * steps + n, 0)),
            kv_spec, kv_spec, kv_spec,
            pl.BlockSpec((3, N_HEADS, WINDOW, 3 * WINDOW), lambda b, n: (0, 0, 0, 0)),
        ],
        out_specs=pl.BlockSpec((rows, ATT_W), lambda b, n: (b * steps + n, 0)),
        out_shape=jax.ShapeDtypeStruct((bsz * seq, ATT_W), BF16),
        scratch_shapes=[pltpu.VMEM((2, N_HEADS, WINDOW, 3 * WINDOW), F32),
                        pltpu.VMEM((2, N_HEADS, WINDOW, 3 * WINDOW), BF16)],
        compiler_params=pltpu.CompilerParams(dimension_semantics=("arbitrary", "arbitrary"),
                                             vmem_limit_bytes=VMEM_LIMIT),
    )(sink[layer], q, k.reshape(bsz, seq, KV_W), v.reshape(bsz, seq, KV_W),
      vs.reshape(bsz, seq, KV_W), bias)


def _attn_bias(seq):
    q_idx = jnp.arange(WINDOW)[:, None]
    c_idx = jnp.arange(3 * WINDOW)[None, :]
    dist = jnp.abs(q_idx - c_idx + WINDOW)
    slopes = jnp.exp2(-8.0 * jnp.arange(1, N_HEADS + 1, dtype=F32) / N_HEADS)
    alibi = -slopes[:, None, None] * dist.astype(F32)[None]
    in_band = (dist <= WINDOW)[None]
    blk = (c_idx // WINDOW)[None]
    variants = []
    for bad in (0, -1, 2):
        ok = in_band & (blk != bad)
        variants.append(jnp.where(ok, alibi * LOG2E, -1e30))
    return jnp.stack(variants).astype(F32)


def _swap_lane_blocks(vs):
    lane = lax.broadcasted_iota(I32, vs[0].shape, 1)
    lo_half = lane < 2 * SUB
    even_blk = (lane & SUB) == 0
    a0, a1, a2, a3 = vs
    c0 = jnp.where(lo_half, a0, pltpu.roll(a2, 2 * SUB, 1))
    c2 = jnp.where(lo_half, pltpu.roll(a0, 2 * SUB, 1), a2)
    c1 = jnp.where(lo_half, a1, pltpu.roll(a3, 2 * SUB, 1))
    c3 = jnp.where(lo_half, pltpu.roll(a1, 2 * SUB, 1), a3)
    b0 = jnp.where(even_blk, c0, pltpu.roll(c1, SUB, 1))
    b1 = jnp.where(even_blk, pltpu.roll(c0, LANES - SUB, 1), c1)
    b2 = jnp.where(even_blk, c2, pltpu.roll(c3, SUB, 1))
    b3 = jnp.where(even_blk, pltpu.roll(c2, LANES - SUB, 1), c3)
    return [b0, b1, b2, b3]


def _ssm_kernel(u_ref, t_ref, ms_ref, mo_ref, pw_ref, dsk_ref, y_ref, *, chunks_per_seq):
    n_grp = u_ref.shape[0] // SSM_CH
    nbc = u_ref.shape[1]
    is_fwd = lax.broadcasted_iota(I32, (nbc, LANES), 1) < SSM_P
    c_idx = lax.broadcasted_iota(I32, (nbc, LANES), 0) & (chunks_per_seq - 1)

    def cmul(x, a):
        return a[0] * x[0] - a[1] * x[1], a[0] * x[1] + a[1] * x[0]

    def cadd(x, y):
        return x[0] + y[0], x[1] + y[1]

    def shifted(x, dd):
        ok_f = jnp.logical_and(is_fwd, c_idx >= dd)
        ok_b = jnp.logical_and(jnp.logical_not(is_fwd), c_idx < chunks_per_seq - dd)
        return tuple(jnp.where(ok_f, pltpu.roll(t, dd, 0), jnp.where(ok_b, pltpu.roll(t, nbc - dd, 0), 0.0))
                     for t in x)

    def gather(g):
        slabs = [u_ref[g * SSM_CH + h] for h in range(SSM_CH)]
        cols = [_swap_lane_blocks(slabs[4 * q:4 * q + 4]) for q in range(SSM_CH // 4)]
        pieces = [jnp.concatenate([cols[q][i] for q in range(SSM_CH // 4)], axis=1)
                  for i in range(N_SUB)]
        return jnp.concatenate(pieces, axis=0)

    def entering_states(g, s_all):
        def prow(r):
            return pw_ref[g, 0, r:r + 1, :], pw_ref[g, 1, r:r + 1, :]

        def mirror(k):
            lo, hi = k * nbc, (N_SUB - 1 - k) * nbc
            return tuple(jnp.where(is_fwd, s_all[lo:lo + nbc, c:c + LANES], s_all[hi:hi + nbc, c:c + LANES])
                         for c in (0, LANES))

        a32 = prow(0)
        sk = [mirror(k) for k in range(N_SUB)]
        zero = jnp.zeros((nbc, LANES), F32)
        pk = [(zero, zero)]
        for k in range(N_SUB - 1):
            pk.append(cadd(cmul(pk[k], a32), sk[k]))
        x = cadd(cmul(pk[N_SUB - 1], a32), sk[N_SUB - 1])
        for kk in range(_N_SCAN_STEPS):
            x = cadd(x, cmul(shifted(x, 1 << kk), prow(3 + kk)))
        ent = shifted(x, 1)
        qk = [cadd(pk[k], cmul(ent, prow(k - 1)) if k > 0 else ent) for k in range(N_SUB)]
        xin = []
        for i in range(N_SUB):
            j = N_SUB - 1 - i
            xin.append(jnp.concatenate([jnp.where(is_fwd, qk[i][c], qk[j][c]) for c in (0, 1)], axis=1))
        return jnp.concatenate(xin, axis=0).astype(BF16)

    def scatter(g, act):
        for q in range(SSM_CH // 4):
            back = _swap_lane_blocks([act[i * nbc:(i + 1) * nbc, q * LANES:(q + 1) * LANES]
                                      for i in range(N_SUB)])
            for r in range(4):
                y_ref[g * SSM_CH + 4 * q + r] = back[r]

    grp = range(n_grp)
    ust = [gather(g) for g in grp]
    ub = [u.astype(BF16) for u in ust]
    z = [_dot(ub[g], t_ref[g]) for g in grp]
    s_all = [_dot(ub[g], ms_ref[g]) for g in grp]
    xin = [entering_states(g, s_all[g]) for g in grp]
    for g in grp:
        y = z[g] + _dot(xin[g], mo_ref[g]) + dsk_ref[g] * ust[g]
        scatter(g, jax.nn.gelu(y))


def _ssm(u3, t_all, ms_all, mo_all, pw_all, dsk_all, layer, chunks_per_seq):
    nbc = u3.shape[1]
    base = layer * (SSM_G // SSM_GB)
    sel3 = lambda g: (base + g, 0, 0)
    slab_spec = pl.BlockSpec((SSM_GB * SSM_CH, nbc, LANES), lambda g: (g, 0, 0))
    return pl.pallas_call(
        functools.partial(_ssm_kernel, chunks_per_seq=chunks_per_seq),
        name="ssm_mix",
        grid=(SSM_G // SSM_GB,),
        in_specs=[
            slab_spec,
            pl.BlockSpec((SSM_GB, CW, CW), sel3),
            pl.BlockSpec((SSM_GB, CW, 2 * LANES), sel3),
            pl.BlockSpec((SSM_GB, 2 * LANES, CW), sel3),
            pl.BlockSpec((SSM_GB, 2, _POW_ROWS, LANES), lambda g: (base + g, 0, 0, 0)),
            pl.BlockSpec((SSM_GB, 1, CW), sel3),
        ],
        out_specs=slab_spec,
        out_shape=jax.ShapeDtypeStruct((SSM_W, nbc, LANES), F32),
        compiler_params=pltpu.CompilerParams(dimension_semantics=("arbitrary",),
                                             vmem_limit_bytes=VMEM_LIMIT),
    )(u3, t_all, ms_all, mo_all, pw_all, dsk_all)


def _stream_cast(pairs, layer, stage_ref, sem_ref):
    slots, rows, cols = stage_ref.shape
    tiles = [(src, dst, r, c) for src, dst in pairs
             for r in range(0, dst.shape[0], rows) for c in range(0, dst.shape[1], cols)]

    def copy(i):
        src, _, r, c = tiles[i]
        return pltpu.make_async_copy(src.at[layer, pl.ds(r, rows), pl.ds(c, cols)],
                                     stage_ref.at[i % slots], sem_ref.at[i % slots])

    for i in range(min(slots - 1, len(tiles))):
        copy(i).start()
    for i, (_, dst, r, c) in enumerate(tiles):
        if i + slots - 1 < len(tiles):
            copy(i + slots - 1).start()
        copy(i).wait()
        dst[r:r + rows, c:c + cols] = stage_ref[i % slots].astype(BF16)


def _outffn_kernel(x_ref, att_ref, y_ref, g2_ref, wglu_hbm, wout_hbm, w1_hbm, w2_hbm, o_ref,
                   wglu_ref, wout_ref, w1_ref, w2_ref, stage_ref, stage_sem, *, layer):
    @pl.when(pl.program_id(0) == 0)
    def _():
        _stream_cast([(wglu_hbm, wglu_ref), (wout_hbm, wout_ref), (w1_hbm, w1_ref), (w2_hbm, w2_ref)],
                     layer, stage_ref, stage_sem)

    first = (pl.program_id(0) % (Y_CHUNKS * LANES // TM_FFN)) * (TM_FFN // LANES)
    y2d = y_ref.reshape(SSM_W * Y_CHUNKS, LANES)
    parts = [slice(r * FFN_PART, (r + 1) * FFN_PART) for r in range(TM_FFN // FFN_PART)]
    per = FFN_PART // LANES
    y = [jnp.concatenate([y2d[pl.ds(first + r * per + j, SSM_W, stride=Y_CHUNKS), :].T
                          for j in range(per)], axis=0).astype(BF16)
         for r in range(len(parts))]
    gl = [_dot(yy, wglu_ref[...]) for yy in y]
    ssm = [(g[:, :SSM_W] * jax.nn.sigmoid(g[:, SSM_W:])).astype(BF16) for g in gl]
    x = [x_ref[rows, :] + _dot(att_ref[rows, :], wout_ref[:ATT_W, :]) + _dot(s, wout_ref[ATT_W:, :])
         for rows, s in zip(parts, ssm)]
    hn = []
    for xx in x:
        ms = jnp.mean(xx * xx, axis=-1, keepdims=True)
        hn.append((xx * lax.rsqrt(ms + EPS) * g2_ref[...]).astype(BF16))
    hid = [jnp.maximum(_dot(h, w1_ref[...]), 0.0) for h in hn]
    for rows, xx, hh in zip(parts, x, hid):
        o_ref[rows, :] = xx + _dot((hh * hh).astype(BF16), w2_ref[...])


def _outffn(x, att, y3, g2, w_glu, w_out, w_ff1, w_ff2, layer):
    t = x.shape[0]
    hbm = pl.BlockSpec(memory_space=pl.ANY)
    return pl.pallas_call(
        functools.partial(_outffn_kernel, layer=layer),
        name="out_ffn",
        grid=(t // TM_FFN,),
        in_specs=[
            pl.BlockSpec((TM_FFN, D_MODEL), lambda i: (i, 0)),
            pl.BlockSpec((TM_FFN, ATT_W), lambda i: (i, 0)),
            pl.BlockSpec((SSM_W, Y_CHUNKS, LANES), lambda i: (0, i * TM_FFN // (Y_CHUNKS * LANES), 0)),
            pl.BlockSpec((None, 1, D_MODEL), lambda i: (layer, 0, 0)),
            hbm, hbm, hbm, hbm,
        ],
        out_specs=pl.BlockSpec((TM_FFN, D_MODEL), lambda i: (i, 0)),
        out_shape=jax.ShapeDtypeStruct((t, D_MODEL), F32),
        scratch_shapes=[
            pltpu.VMEM((SSM_W, 2 * SSM_W), BF16),
            pltpu.VMEM((D_MODEL, D_MODEL), BF16),
            pltpu.VMEM((D_MODEL, D_FF), BF16),
            pltpu.VMEM((D_FF, D_MODEL), BF16),
            pltpu.VMEM((STAGE_SLOTS, STAGE_ROWS, D_MODEL), F32),
            pltpu.SemaphoreType.DMA((STAGE_SLOTS,)),
        ],
        compiler_params=pltpu.CompilerParams(dimension_semantics=("arbitrary",),
                                             vmem_limit_bytes=VMEM_LIMIT),
    )(x, att, y3, g2, w_glu, w_out, w_ff1, w_ff2)


def _prep_operands(lam_re, lam_im, log_dt, b_re, b_im, c_re, c_im, d_skip):
    depth = lam_re.shape[0]
    lg = depth * SSM_G
    both = lambda a: jnp.concatenate([a[:, 0], a[:, 1]], axis=-1)
    ldt = jnp.broadcast_to(log_dt[..., None], lam_re.shape)
    lam = jnp.stack([both(lam_re), both(lam_im), both(ldt)], axis=2).reshape(lg, 3, LANES)
    bt = jnp.stack([b_re, b_im], axis=2).transpose(0, 1, 2, 4, 3)
    bt = jnp.concatenate([bt, bt], axis=-1).reshape(lg, 2, SSM_CH, LANES)
    cc = jnp.stack([both(c_re), both(c_im)], axis=2).reshape(lg, 2, SSM_CH, LANES)
    dsk = jnp.repeat(d_skip.reshape(lg, 1, SSM_CH), SUB, axis=-1)
    return list(_ssm_prep(lam, bt, cc)) + [dsk]


def kernel(x, norm1, w_in, q_gain, k_gain, sink, lam_re, lam_im, log_dt, b_re, b_im, c_re, c_im,
           d_skip, w_glu, w_out, norm2, w_ff1, w_ff2):
    bsz, seq, _ = x.shape
    depth = norm1.shape[0]
    tokens = bsz * seq
    chunks_per_seq = seq // LANES
    assert seq % (ATT_NB * WINDOW) == 0 and tokens % TM_IN == 0 and tokens % TM_FFN == 0
    assert TM_IN == Y_CHUNKS * LANES and (Y_CHUNKS * LANES) % TM_FFN == 0
    assert chunks_per_seq == 1 << _N_SCAN_STEPS and max(_POW_N) == LANES * chunks_per_seq // 2

    t_all, ms_all, mo_all, pw_all, dsk_all = _prep_operands(lam_re, lam_im, log_dt, b_re, b_im,
                                                           c_re, c_im, d_skip)
    g1 = norm1[:, None, :]
    g2 = norm2[:, None, :]
    qg = (jnp.tile(q_gain, (1, N_HEADS)) * (LOG2E / math.sqrt(HEAD_DIM)))[:, None, :]
    kg = jnp.tile(k_gain, (1, N_KV))[:, None, :]
    bias = _attn_bias(seq)

    xt = x.reshape(tokens, D_MODEL)
    for layer in range(depth):
        q, k, v, vs, u3 = _inproj(xt, g1, w_in, qg, kg, layer)
        att = _attention(sink, q, k, v, vs, bias, bsz, seq, layer)
        y3 = _ssm(u3, t_all, ms_all, mo_all, pw_all, dsk_all, layer, chunks_per_seq)
        xt = _outffn(xt, att, y3, g2, w_glu, w_out, w_ff1, w_ff2, layer)
    return xt.reshape(bsz, seq, D_MODEL)
```
